```python
import math
import jax, jax.numpy as jnp
from jax import lax
import numpy as np

D_MODEL = 1024
BATCH = 2
SEQ = 8192
DEPTH = 1
DEC_BATCH = 128
DEC_SEQ = 8
PAST_LEN = 8192
PAGE_SIZE = 128

SB_HEADS = 8
SB_HEAD_DIM = 64
SB_WIDTH = SB_HEADS * SB_HEAD_DIM
SB_SCALE = SB_HEAD_DIM ** -0.5
SB_BIAS_INIT = -8.0
Q_BLOCK = 128
SSD_HEADS = 8
SSD_HEAD_DIM = 64
SSD_WIDTH = SSD_HEADS * SSD_HEAD_DIM
SSD_GROUPS = 2
SSD_HPG = SSD_HEADS // SSD_GROUPS
SSD_STATE = 128
SSD_CONV = 4
SSD_CHUNK = 128
CONV_DIM = SSD_WIDTH + 2 * SSD_GROUPS * SSD_STATE
MIX_WIDTH = SB_WIDTH + SSD_WIDTH
IN_PROJ = 3 * SB_WIDTH + SSD_WIDTH + CONV_DIM + SSD_HEADS
N_EXPERTS = 32
TOP_K = 4
D_EXPERT = D_MODEL
SWIGLU_LIMIT = 7.0
SWIGLU_ALPHA = 1.702
PLE_DIM = 256
EPS = 1e-6

kernel_name = 'hymba_stickbreak_ssd_moe_step'


def rmsnorm(x, g):
    xf = x.astype(jnp.float32)
    y = xf * lax.rsqrt(jnp.mean(xf * xf, axis=-1, keepdims=True) + EPS)
    return (y * g.astype(jnp.float32)).astype(x.dtype)


def sb_attend(q, k, v, bias, q_pos, k_pos):
    z = (jnp.einsum('bqhd,bkhd->bhqk', q, k).astype(jnp.float32) * SB_SCALE
         + bias.astype(jnp.float32)[None, :, None, None])
    mask = k_pos[None, :] < q_pos[:, None]
    log_not = jnp.where(mask, jax.nn.log_sigmoid(-z), 0.0)
    between = lax.cumsum(log_not, axis=3, reverse=True) - log_not
    w = jnp.where(mask, jnp.exp(jax.nn.log_sigmoid(z) + between), 0.0)
    return jnp.einsum('bhqk,bkhd->bqhd', w.astype(v.dtype), v)


def causal_conv(u, prev, w, b):
    t = u.shape[1]
    full = jnp.concatenate([prev.astype(u.dtype), u], axis=1)
    out = sum(full[:, j:j + t] * w[j] for j in range(SSD_CONV)) + b
    return out, full[:, -(SSD_CONV - 1):]


def ssd_scan(x, dt, A, bm, cm, h0):
    bsz, t = x.shape[:2]
    chunk = min(SSD_CHUNK, t)
    pad = (-t) % chunk
    if pad:
        padt = lambda a: jnp.pad(a, [(0, 0), (0, pad)] + [(0, 0)] * (a.ndim - 2))
        x, dt, bm, cm = padt(x), padt(dt), padt(bm), padt(cm)
    nc = (t + pad) // chunk
    x = x.reshape(bsz, nc, chunk, SSD_GROUPS, SSD_HPG, SSD_HEAD_DIM)
    dt = dt.reshape(bsz, nc, chunk, SSD_GROUPS, SSD_HPG)
    bm = bm.reshape(bsz, nc, chunk, SSD_GROUPS, SSD_STATE)
    cm = cm.reshape(bsz, nc, chunk, SSD_GROUPS, SSD_STATE)
    xdt = x * dt[..., None]
    acs = jnp.cumsum(dt * A, axis=2)
    causal = jnp.tril(jnp.ones((chunk, chunk), bool))[:, :, None, None]
    seg = acs[:, :, :, None] - acs[:, :, None, :]
    lmat = jnp.exp(jnp.where(causal, seg, -jnp.inf))
    cb = jnp.einsum('bclgn,bcsgn->bclsg', cm, bm)
    y_diag = jnp.einsum('bclsgr,bcsgrp->bclgrp', cb[..., None] * lmat, xdt)
    decay_in = jnp.exp(acs[:, :, -1:] - acs)
    states = jnp.einsum('bclgn,bclgr,bclgrp->bcgrpn', bm, decay_in, xdt)
    chunk_decay = jnp.exp(acs[:, :, -1])

    def step(h, inp):
        dec, st = inp
        return dec[..., None, None] * h + st, h

    h_fin, h_prev = lax.scan(step, h0.astype(jnp.float32),
                             (jnp.moveaxis(chunk_decay, 1, 0), jnp.moveaxis(states, 1, 0)))
    h_prev = jnp.moveaxis(h_prev, 0, 1)
    y_off = jnp.einsum('bclgn,bcgrpn,bclgr->bclgrp', cm, h_prev, jnp.exp(acs))
    y = (y_diag + y_off).reshape(bsz, nc * chunk, SSD_GROUPS, SSD_HPG, SSD_HEAD_DIM)[:, :t]
    return y, h_fin


def moe(x, w_router, b_router, w_gate_up, b_gate_up, w_down, b_down):
    logits = (x @ w_router + b_router).astype(jnp.float32)
    top_val, top_idx = lax.top_k(logits, TOP_K)
    gates = jax.nn.softmax(top_val, axis=-1)
    comb = jnp.sum(jax.nn.one_hot(top_idx, N_EXPERTS, dtype=jnp.float32) * gates[..., None], axis=1)
    out = jnp.zeros(x.shape, jnp.float32)
    for e in range(N_EXPERTS):
        gu = x @ w_gate_up[e] + b_gate_up[e]
        gate = jnp.minimum(gu[:, 0::2], SWIGLU_LIMIT)
        up = jnp.clip(gu[:, 1::2], -SWIGLU_LIMIT, SWIGLU_LIMIT)
        act = (up + 1.0) * (gate * jax.nn.sigmoid(SWIGLU_ALPHA * gate))
        out = out + comb[:, e:e + 1] * (act @ w_down[e] + b_down[e])
    return out.astype(x.dtype)


def mixer(a, w_in, w_out, g_attn, sb_bias, conv_w, conv_b, dt_bias, a_log, d_skip, g_ssd,
          past_k, past_v, ssm0, conv0):
    bsz, t = a.shape[:2]
    proj = a @ w_in
    cuts = np.cumsum([SB_WIDTH, SB_WIDTH, SB_WIDTH, SSD_WIDTH, CONV_DIM]).tolist()
    q, k, v, z, xbc, dt_raw = jnp.split(proj, cuts, axis=-1)
    q = q.reshape(bsz, t, SB_HEADS, SB_HEAD_DIM)
    k = k.reshape(bsz, t, SB_HEADS, SB_HEAD_DIM)
    v = v.reshape(bsz, t, SB_HEADS, SB_HEAD_DIM)
    if past_k is None:
        blocks = []
        for s0 in range(0, t, Q_BLOCK):
            e = min(s0 + Q_BLOCK, t)
            blocks.append(sb_attend(q[:, s0:e], k[:, :e], v[:, :e], sb_bias,
                                    jnp.arange(s0, e), jnp.arange(e)))
        o = jnp.concatenate(blocks, axis=1)
    else:
        n_past = past_k.shape[1]
        kk = jnp.concatenate([past_k.astype(k.dtype), k], axis=1)
        vv = jnp.concatenate([past_v.astype(v.dtype), v], axis=1)
        o = sb_attend(q, kk, vv, sb_bias, n_past + jnp.arange(t), jnp.arange(n_past + t))
    attn = rmsnorm(o.reshape(bsz, t, SB_WIDTH), g_attn)
    u, conv_new = causal_conv(xbc, conv0, conv_w, conv_b)
    u = jax.nn.silu(u)
    nb = SSD_GROUPS * SSD_STATE
    xs = u[..., :SSD_WIDTH].reshape(bsz, t, SSD_GROUPS, SSD_HPG, SSD_HEAD_DIM)
    bm = u[..., SSD_WIDTH:SSD_WIDTH + nb].reshape(bsz, t, SSD_GROUPS, SSD_STATE)
    cm = u[..., SSD_WIDTH + nb:].reshape(bsz, t, SSD_GROUPS, SSD_STATE)
    dt = jax.nn.softplus(dt_raw.astype(jnp.float32) + dt_bias.astype(jnp.float32))
    dt = dt.reshape(bsz, t, SSD_GROUPS, SSD_HPG)
    A = -jnp.exp(a_log.astype(jnp.float32)).reshape(SSD_GROUPS, SSD_HPG)
    h0 = ssm0.reshape(bsz, SSD_GROUPS, SSD_HPG, SSD_HEAD_DIM, SSD_STATE)
    y, h_fin = ssd_scan(xs, dt, A, bm, cm, h0)
    y = y + d_skip.astype(jnp.float32).reshape(SSD_GROUPS, SSD_HPG)[..., None] * xs
    gw = SSD_HPG * SSD_HEAD_DIM
    y = y.reshape(bsz, t, SSD_GROUPS, gw) * jax.nn.silu(z.astype(jnp.float32)).reshape(bsz, t, SSD_GROUPS, gw)
    y = y * lax.rsqrt(jnp.mean(y * y, axis=-1, keepdims=True) + EPS)
    y = (y.reshape(bsz, t, SSD_WIDTH) * g_ssd.astype(jnp.float32)).astype(a.dtype)
    out = jnp.concatenate([attn, y], axis=-1) @ w_out
    return out, k, v, h_fin.reshape(bsz, SSD_HEADS, SSD_HEAD_DIM, SSD_STATE), conv_new


def setup_inputs(seed: int = 0) -> dict:
    key = jax.random.key(seed)
    ks = iter(jax.random.split(key, 40))
    nrm = lambda shape, s=1.0: jax.random.normal(next(ks), shape, jnp.float32) * s
    gain = lambda shape: 1.0 + 0.01 * jax.random.normal(next(ks), shape, jnp.float32)
    n_pages = PAST_LEN // PAGE_SIZE
    n_pool = (5 * DEC_BATCH * n_pages) // 4
    page_table = jax.random.permutation(next(ks), n_pool)[:DEC_BATCH * n_pages]
    page_table = page_table.reshape(DEC_BATCH, n_pages).astype(jnp.int32)
    dt0 = jnp.exp(jax.random.uniform(next(ks), (DEPTH, SSD_HEADS), jnp.float32,
                                     math.log(1e-3), math.log(1e-1)))
    dt_bias = dt0 + jnp.log(-jnp.expm1(-dt0))
    a_log = jnp.log(jax.random.uniform(next(ks), (DEPTH, SSD_HEADS), jnp.float32, 1.0, 16.0))
    return {
        'x_prompt': nrm((BATCH, SEQ, D_MODEL)),
        'x_sample': nrm((DEC_BATCH, DEC_SEQ, D_MODEL)),
        'cache_k': nrm((DEPTH, n_pool, PAGE_SIZE, SB_HEADS, SB_HEAD_DIM)),
        'cache_v': nrm((DEPTH, n_pool, PAGE_SIZE, SB_HEADS, SB_HEAD_DIM)),
        'state_ssm': nrm((DEPTH, DEC_BATCH, SSD_HEADS, SSD_HEAD_DIM, SSD_STATE), 0.5),
        'state_conv': nrm((DEPTH, DEC_BATCH, SSD_CONV - 1, CONV_DIM)),
        'page_table': page_table,
        'p_prompt': nrm((DEPTH, BATCH, SEQ, PLE_DIM)),
        'p_sample': nrm((DEPTH, DEC_BATCH, DEC_SEQ, PLE_DIM)),
        'g_mix': gain((DEPTH, D_MODEL)),
        'w_in': nrm((DEPTH, D_MODEL, IN_PROJ), D_MODEL ** -0.5),
        'w_out': nrm((DEPTH, MIX_WIDTH, D_MODEL), MIX_WIDTH ** -0.5),
        'g_attn': gain((DEPTH, SB_WIDTH)),
        'sb_bias': SB_BIAS_INIT + 0.1 * jax.random.normal(next(ks), (DEPTH, SB_HEADS), jnp.float32),
        'conv_w': nrm((DEPTH, SSD_CONV, CONV_DIM), SSD_CONV ** -0.5),
        'conv_b': nrm((DEPTH, CONV_DIM), 0.02),
        'dt_bias': dt_bias,
        'a_log': a_log,
        'd_skip': gain((DEPTH, SSD_HEADS)),
        'g_ssd': gain((DEPTH, SSD_WIDTH)),
        'g_ffn': gain((DEPTH, D_MODEL)),
        'w_router': nrm((DEPTH, D_MODEL, N_EXPERTS), D_MODEL ** -0.5),
        'b_router': nrm((DEPTH, N_EXPERTS), 0.01),
        'w_gate_up': nrm((DEPTH, N_EXPERTS, D_MODEL, 2 * D_EXPERT), D_MODEL ** -0.5),
        'b_gate_up': nrm((DEPTH, N_EXPERTS, 2 * D_EXPERT), 0.01),
        'w_down': nrm((DEPTH, N_EXPERTS, D_EXPERT, D_MODEL), D_EXPERT ** -0.5),
        'b_down': nrm((DEPTH, N_EXPERTS, D_MODEL), 0.01),
        'g_ple_in': gain((DEPTH, D_MODEL)),
        'w_ple_gate': nrm((DEPTH, D_MODEL, D_MODEL), D_MODEL ** -0.5),
        'w_ple_proj': nrm((DEPTH, PLE_DIM, D_MODEL), PLE_DIM ** -0.5),
        'g_ple_out': gain((DEPTH, D_MODEL)),
        'g_final': gain((D_MODEL,)),
    }


def reference(x_prompt, x_sample, cache_k, cache_v, state_ssm, state_conv, page_table,
              p_prompt, p_sample, g_mix, w_in, w_out, g_attn, sb_bias, conv_w, conv_b, dt_bias,
              a_log, d_skip, g_ssd, g_ffn, w_router, b_router, w_gate_up, b_gate_up, w_down,
              b_down, g_ple_in, w_ple_gate, w_ple_proj, g_ple_out, g_final):

    def block(h, p_l, l, past_k, past_v, ssm0, conv0):
        a = rmsnorm(h, g_mix[l])
        mix, k_new, v_new, ssm_new, conv_new = mixer(
            a, w_in[l], w_out[l], g_attn[l], sb_bias[l], conv_w[l], conv_b[l], dt_bias[l],
            a_log[l], d_skip[l], g_ssd[l], past_k, past_v, ssm0, conv0)
        h = h + mix
        f = rmsnorm(h, g_ffn[l])
        h = h + moe(f.reshape(-1, D_MODEL), w_router[l], b_router[l], w_gate_up[l],
                    b_gate_up[l], w_down[l], b_down[l]).reshape(h.shape)
        c = rmsnorm(h, g_ple_in[l])
        gate = jax.nn.sigmoid(c @ w_ple_gate[l])
        emb = rmsnorm(p_l @ w_ple_proj[l], g_ple_out[l])
        h = h + gate * emb
        return h, k_new, v_new, ssm_new, conv_new

    bp = x_prompt.shape[0]
    bs = x_sample.shape[0]
    h_p, h_s = x_prompt, x_sample
    kp, vp, sp, cp, kq, vq, sq, cq = [], [], [], [], [], [], [], []
    for l in range(DEPTH):
        ssm0 = jnp.zeros((bp, SSD_HEADS, SSD_HEAD_DIM, SSD_STATE), jnp.float32)
        conv0 = jnp.zeros((bp, SSD_CONV - 1, CONV_DIM), x_prompt.dtype)
        h_p, k1, v1, s1, c1 = block(h_p, p_prompt[l], l, None, None, ssm0, conv0)
        past_k = cache_k[l][page_table].reshape(bs, -1, SB_HEADS, SB_HEAD_DIM)
        past_v = cache_v[l][page_table].reshape(bs, -1, SB_HEADS, SB_HEAD_DIM)
        h_s, k2, v2, s2, c2 = block(h_s, p_sample[l], l, past_k, past_v, state_ssm[l], state_conv[l])
        kp.append(k1); vp.append(v1); sp.append(s1); cp.append(c1)
        kq.append(k2); vq.append(v2); sq.append(s2); cq.append(c2)
    y_prompt = rmsnorm(h_p, g_final)
    y_sample = rmsnorm(h_s, g_final)
    return (y_prompt, y_sample, jnp.stack(kp), jnp.stack(vp), jnp.stack(sp), jnp.stack(cp),
            jnp.stack(kq), jnp.stack(vq), jnp.stack(sq), jnp.stack(cq))
```

```python
import functools

import jax
import jax.numpy as jnp
from jax import lax
from jax.experimental import pallas as pl
from jax.experimental.pallas import tpu as pltpu

F32 = jnp.float32
BF16 = jnp.bfloat16

D_MODEL = 1024
SB_HEADS = 8
SB_HEAD_DIM = 64
SB_WIDTH = SB_HEADS * SB_HEAD_DIM
SB_SCALE = SB_HEAD_DIM ** -0.5
SSD_HEADS = 8
SSD_HEAD_DIM = 64
SSD_WIDTH = SSD_HEADS * SSD_HEAD_DIM
SSD_GROUPS = 2
SSD_STATE = 128
SSD_CONV = 4
CONV_DIM = SSD_WIDTH + 2 * SSD_GROUPS * SSD_STATE
N_EXPERTS = 32
TOP_K = 4
D_EXPERT = D_MODEL
SWIGLU_LIMIT = 7.0
SWIGLU_ALPHA = 1.702
PLE_DIM = 256
EPS = 1e-6

LANES = 128
SUBLANES = 8
VMEM_LIMIT = 56 * 1024 * 1024
NEG_BIG = -1e30

_DT_PAD = LANES
_IN_CUTS = (SB_WIDTH, SB_WIDTH, SB_WIDTH, SSD_WIDTH, CONV_DIM, _DT_PAD)
_IN_COLS = sum(_IN_CUTS)

_NT = (((1,), (1,)), ((), ()))
_TN = (((0,), (0,)), ((), ()))


def _row_tile(n, tm):
    tm = min(tm, n)
    while n % tm:
        tm -= SUBLANES
    return tm


def _cparams(sem):
    return pltpu.CompilerParams(dimension_semantics=sem, vmem_limit_bytes=VMEM_LIMIT)


def _rms(x, g):
    return x * lax.rsqrt(jnp.mean(x * x, axis=-1, keepdims=True) + EPS) * g


def _split3(x):
    hi = x.astype(BF16)
    r1 = x - hi.astype(F32)
    mid = r1.astype(BF16)
    lo = (r1 - mid.astype(F32)).astype(BF16)
    return hi, mid, lo


def _dot3_l(x, m):
    hi, mid, lo = _split3(x)
    d = lambda a: jnp.dot(a, m, preferred_element_type=F32)
    return d(hi) + d(mid) + d(lo)


def _dot3_r(m, x):
    hi, mid, lo = _split3(x)
    d = lambda a: jnp.dot(m, a, preferred_element_type=F32)
    return d(hi) + d(mid) + d(lo)


def _inproj_kernel(x_ref, g_ref, w_ref, q_ref, k_ref, v_ref, z_ref, xbc_ref, dt_ref):
    a = _rms(x_ref[...], g_ref[...]).astype(BF16)
    c0 = 0
    for ref, width in zip((q_ref, k_ref, v_ref, z_ref, xbc_ref, dt_ref), _IN_CUTS):
        ref[...] = jnp.dot(a, w_ref[:, c0:c0 + width], preferred_element_type=F32)
        c0 += width


def _inproj(x, g, w_pad, tm=256):
    n = x.shape[0]
    tm = _row_tile(n, tm)
    row = lambda i: (i, 0)
    fixed = lambda i: (0, 0)
    return pl.pallas_call(
        _inproj_kernel,
        grid=(n // tm,),
        in_specs=[pl.BlockSpec((tm, D_MODEL), row),
                  pl.BlockSpec((1, D_MODEL), fixed),
                  pl.BlockSpec((D_MODEL, _IN_COLS), fixed)],
        out_specs=[pl.BlockSpec((tm, c), row) for c in _IN_CUTS],
        out_shape=[jax.ShapeDtypeStruct((n, c), F32) for c in _IN_CUTS],
        compiler_params=_cparams(("parallel",)),
        name="inproj",
    )(x, g, w_pad)


def _neg_softplus_parts(z):
    l1p = jnp.log1p(jnp.exp(-jnp.abs(z)))
    sp = jnp.maximum(z, 0.0) + l1p
    return -sp, z - sp


def _attn_prompt_kernel(bias_ref, q_ref, k_ref, v_ref, tri_ref, o_ref, acc_ref, car_ref, *, blk):
    hp = pl.program_id(1)
    qi = pl.program_id(2)
    lane = lax.broadcasted_iota(jnp.int32, (blk, LANES), 1)
    row = lax.broadcasted_iota(jnp.int32, (blk, blk), 0)
    col = lax.broadcasted_iota(jnp.int32, (blk, blk), 1)
    causal = col < row
    q2 = q_ref[0] * SB_SCALE
    tri = tri_ref[...]
    outs = []
    for hh in range(2):
        in_head = (lane < SB_HEAD_DIM) if hh == 0 else (lane >= SB_HEAD_DIM)
        qh = jnp.where(in_head, q2, 0.0).astype(BF16)
        bias = bias_ref[hp * 2 + hh]

        def block(kj, masked):
            start = pl.multiple_of(kj * blk, blk)
            kb = k_ref[0, pl.ds(start, blk), :].astype(BF16)
            vb = v_ref[0, pl.ds(start, blk), :].astype(BF16)
            z = lax.dot_general(qh, kb, _NT, preferred_element_type=F32) + bias
            log_not, log_sig = _neg_softplus_parts(z)
            if masked:
                log_not = jnp.where(causal, log_not, 0.0)
            between = jnp.dot(log_not.astype(BF16), tri, preferred_element_type=F32) + car_ref[...]
            w = jnp.exp(log_sig + between)
            if masked:
                w = jnp.where(causal, w, 0.0)
            acc_ref[...] += jnp.dot(w.astype(BF16), vb, preferred_element_type=F32)
            car_ref[...] += jnp.sum(log_not, axis=-1, keepdims=True)

        acc_ref[...] = jnp.zeros_like(acc_ref)
        car_ref[...] = jnp.zeros_like(car_ref)
        block(qi, True)

        def body(t, carry):
            block(qi - 1 - t, False)
            return carry

        lax.fori_loop(0, qi, body, 0)
        outs.append(acc_ref[...])
    o_ref[0] = jnp.where(lane < SB_HEAD_DIM, outs[0], outs[1])


def _strict_upper_sum_matrix(n):
    j = lax.broadcasted_iota(jnp.int32, (n, n), 0)
    s = lax.broadcasted_iota(jnp.int32, (n, n), 1)
    return (j > s).astype(BF16)


def _attn_prompt(q, k, v, bias, blk=256):
    b, t, _ = q.shape
    blk = min(blk, t)
    tri = _strict_upper_sum_matrix(blk)
    grid_spec = pltpu.PrefetchScalarGridSpec(
        num_scalar_prefetch=0,
        grid=(b, SB_WIDTH // LANES, t // blk),
        in_specs=[pl.BlockSpec(memory_space=pltpu.SMEM),
                  pl.BlockSpec((1, blk, LANES), lambda bi, h, i: (bi, i, h)),
                  pl.BlockSpec((1, t, LANES), lambda bi, h, i: (bi, 0, h)),
                  pl.BlockSpec((1, t, LANES), lambda bi, h, i: (bi, 0, h)),
                  pl.BlockSpec((blk, blk), lambda bi, h, i: (0, 0))],
        out_specs=pl.BlockSpec((1, blk, LANES), lambda bi, h, i: (bi, i, h)),
        scratch_shapes=[pltpu.VMEM((blk, LANES), F32), pltpu.VMEM((blk, 1), F32)],
    )
    return pl.pallas_call(
        functools.partial(_attn_prompt_kernel, blk=blk),
        grid_spec=grid_spec,
        out_shape=jax.ShapeDtypeStruct((b, t, SB_WIDTH), F32),
        compiler_params=_cparams(("parallel", "parallel", "arbitrary")),
        name="attn_prompt",
    )(bias, q, k, v, tri)


def _attn_sample_kernel(pt_ref, q_ref, kn_ref, vn_ref, bias_ref, tri_ref, *rest, pps, page, tq):
    del pt_ref
    kp_refs = rest[:pps]
    vp_refs = rest[pps:2 * pps]
    o_ref = rest[2 * pps]
    qbd_ref, acc_ref, car_ref = rest[2 * pps + 1:]
    j = pl.program_id(1)
    nj = pl.num_programs(1)
    nq = SB_HEADS * tq

    def attend(kb, vb, mask):
        z = lax.dot_general(kb, qbd_ref[...], _NT, preferred_element_type=F32) + bias_ref[...]
        log_not, log_sig = _neg_softplus_parts(z)
        if mask is not None:
            log_not = jnp.where(mask, log_not, 0.0)
        between = jnp.dot(tri_ref[...], log_not.astype(BF16), preferred_element_type=F32) + car_ref[...]
        w = jnp.exp(log_sig + between)
        if mask is not None:
            w = jnp.where(mask, w, 0.0)
        acc_ref[...] += lax.dot_general(w.astype(BF16), vb, _TN, preferred_element_type=F32)
        car_ref[...] += jnp.sum(log_not, axis=0, keepdims=True)

    @pl.when(j == 0)
    def _():
        q = q_ref[0] * SB_SCALE
        qt = jnp.concatenate([q] * SB_HEADS + [jnp.zeros((LANES - nq, SB_WIDTH), F32)], axis=0)
        r = lax.broadcasted_iota(jnp.int32, (LANES, SB_WIDTH), 0)
        c = lax.broadcasted_iota(jnp.int32, (LANES, SB_WIDTH), 1)
        keep = (r // tq == c // SB_HEAD_DIM) & (r < nq)
        qbd_ref[...] = jnp.where(keep, qt, 0.0).astype(BF16)
        acc_ref[...] = jnp.zeros_like(acc_ref)
        car_ref[...] = jnp.zeros_like(car_ref)
        s = lax.broadcasted_iota(jnp.int32, (page, LANES), 0)
        tcol = lax.broadcasted_iota(jnp.int32, (page, LANES), 1) % tq
        fill = jnp.zeros((page - tq, SB_WIDTH), F32)
        kn = jnp.concatenate([kn_ref[0], fill], axis=0).astype(BF16)
        vn = jnp.concatenate([vn_ref[0], fill], axis=0).astype(BF16)
        attend(kn, vn, s < tcol)

    for i in range(pps):
        attend(kp_refs[i][0].astype(BF16), vp_refs[i][0].astype(BF16), None)

    @pl.when(j == nj - 1)
    def _():
        lane = lax.broadcasted_iota(jnp.int32, (tq, SB_WIDTH), 1)
        out = jnp.zeros((tq, SB_WIDTH), F32)
        for h in range(SB_HEADS):
            rows = acc_ref[h * tq:(h + 1) * tq, :]
            out = jnp.where(lane // SB_HEAD_DIM == h, rows, out)
        o_ref[0] = out


def _attn_sample(q, k_new, v_new, cache_k, cache_v, page_table, bias, pps=8):
    bs, tq, _ = q.shape
    n_pool, page = cache_k.shape[0], cache_k.shape[1]
    n_pages = page_table.shape[1]
    pps = min(pps, n_pages)
    assert n_pages % pps == 0 and SB_HEADS * tq <= LANES and tq % SUBLANES == 0
    ck = cache_k.reshape(n_pool, page, SB_WIDTH)
    cv = cache_v.reshape(n_pool, page, SB_WIDTH)
    bias_cols = jnp.concatenate([jnp.repeat(bias, tq), jnp.zeros((LANES - SB_HEADS * tq,), F32)])[None, :]
    r = lax.broadcasted_iota(jnp.int32, (page, page), 0)
    c = lax.broadcasted_iota(jnp.int32, (page, page), 1)
    tri = (c > r).astype(BF16)

    def page_spec(i):
        return pl.BlockSpec((1, page, SB_WIDTH),
                            lambda b, j, pt: (pt[b, n_pages - 1 - (j * pps + i)], 0, 0))

    per_seq = pl.BlockSpec((1, tq, SB_WIDTH), lambda b, j, pt: (b, 0, 0))
    grid_spec = pltpu.PrefetchScalarGridSpec(
        num_scalar_prefetch=1,
        grid=(bs, n_pages // pps),
        in_specs=[per_seq, per_seq, per_seq,
                  pl.BlockSpec((1, LANES), lambda b, j, pt: (0, 0)),
                  pl.BlockSpec((page, page), lambda b, j, pt: (0, 0))]
                 + [page_spec(i) for i in range(pps)] * 2,
        out_specs=per_seq,
        scratch_shapes=[pltpu.VMEM((LANES, SB_WIDTH), BF16),
                        pltpu.VMEM((LANES, SB_WIDTH), F32),
                        pltpu.VMEM((1, LANES), F32)],
    )
    return pl.pallas_call(
        functools.partial(_attn_sample_kernel, pps=pps, page=page, tq=tq),
        grid_spec=grid_spec,
        out_shape=jax.ShapeDtypeStruct((bs, tq, SB_WIDTH), F32),
        compiler_params=_cparams(("parallel", "arbitrary")),
        name="attn_sample",
    )(page_table, q, k_new, v_new, bias_cols, tri, *([ck] * pps), *([cv] * pps))


def _ssd_kernel(xbc_ref, dt_ref, h0_ref, c0_ref, cw_ref, cb_ref, dtb_ref, a_ref, dsk_ref,
                tril_ref, triu_ref, y_ref, hout_ref, xext_ref, h_ref, *, chunk, tin):
    ci = pl.program_id(1)
    nc = pl.num_programs(1)
    pad = SUBLANES

    @pl.when(ci == 0)
    def _():
        h_ref[...] = h0_ref[0]
        xext_ref[0:pad, :] = c0_ref[0]

    @pl.when(ci > 0)
    def _():
        xext_ref[0:pad, :] = xext_ref[chunk:chunk + pad, :]

    if tin < chunk:
        xext_ref[pad:, :] = jnp.zeros((chunk, CONV_DIM), F32)
    xext_ref[pad:pad + tin, :] = xbc_ref[0]

    conv = cb_ref[...] + sum(
        xext_ref[pad - (SSD_CONV - 1) + jj: pad - (SSD_CONV - 1) + jj + chunk, :] * cw_ref[jj:jj + 1, :]
        for jj in range(SSD_CONV))
    u = conv * jax.nn.sigmoid(conv)

    if tin < chunk:
        dt_in = jnp.concatenate([dt_ref[0], jnp.zeros((chunk - tin, LANES), F32)], axis=0)
    else:
        dt_in = dt_ref[0]
    dt = jax.nn.softplus(dt_in + dtb_ref[...])
    if tin < chunk:
        live = lax.broadcasted_iota(jnp.int32, (chunk, LANES), 0) < tin
        dt = jnp.where(live, dt, 0.0)
    a = dt * a_ref[...]
    acs = _dot3_r(tril_ref[...], a)
    acs_t = _dot3_l(a.T, triu_ref[...])
    acs_last = acs[chunk - 1:chunk, :]
    decay_in = jnp.exp(acs_last - acs)
    e_acs = jnp.exp(acs)
    cdec_t = jnp.exp(acs_t[:, chunk - 1:chunk])

    lrow = lax.broadcasted_iota(jnp.int32, (chunk, chunk), 0)
    scol = lax.broadcasted_iota(jnp.int32, (chunk, chunk), 1)
    causal = scol <= lrow
    lane = lax.broadcasted_iota(jnp.int32, (chunk, LANES), 1)
    first = lane < SSD_HEAD_DIM
    srow = lax.broadcasted_iota(jnp.int32, (LANES, LANES), 0)

    hpg = SSD_HEADS // SSD_GROUPS
    for g in range(SSD_GROUPS):
        b0 = SSD_WIDTH + g * SSD_STATE
        c0 = SSD_WIDTH + SSD_GROUPS * SSD_STATE + g * SSD_STATE
        bm = u[:, b0:b0 + SSD_STATE].astype(BF16)
        cm = u[:, c0:c0 + SSD_STATE].astype(BF16)
        cb = lax.dot_general(cm, bm, _NT, preferred_element_type=F32)
        for pr in range(hpg // 2):
            h0i = g * hpg + 2 * pr
            h1i = h0i + 1
            x2 = u[:, h0i * SSD_HEAD_DIM:(h0i + 2) * SSD_HEAD_DIM]
            pick = lambda m: jnp.where(first, m[:, h0i:h0i + 1], m[:, h1i:h1i + 1])
            xdt = x2 * pick(dt)
            xdt_b = xdt.astype(BF16)
            ys = []
            for hi in (h0i, h1i):
                seg = acs[:, hi:hi + 1] - acs_t[hi:hi + 1, :]
                lmat = jnp.exp(jnp.where(causal, seg, -jnp.inf))
                ys.append(jnp.dot((cb * lmat).astype(BF16), xdt_b, preferred_element_type=F32))
            y_diag = jnp.where(first, ys[0], ys[1])
            hrows = slice(h0i * SSD_HEAD_DIM, (h0i + 2) * SSD_HEAD_DIM)
            h_prev = h_ref[hrows, :]
            y_off = lax.dot_general(cm, h_prev.astype(BF16), _NT, preferred_element_type=F32) * pick(e_acs)
            states = lax.dot_general((xdt * pick(decay_in)).astype(BF16), bm, _TN,
                                     preferred_element_type=F32)
            cdec = jnp.where(srow < SSD_HEAD_DIM, cdec_t[h0i:h0i + 1, :], cdec_t[h1i:h1i + 1, :])
            h_ref[hrows, :] = cdec * h_prev + states
            y2 = y_diag + y_off + dsk_ref[:, h0i * SSD_HEAD_DIM:(h0i + 2) * SSD_HEAD_DIM] * x2
            y_ref[0, :, h0i * SSD_HEAD_DIM:(h0i + 2) * SSD_HEAD_DIM] = y2[0:tin, :]

    @pl.when(ci == nc - 1)
    def _():
        hout_ref[0] = h_ref[...]


def _ssd(xbc, dt_raw, h0, conv0, conv_w, conv_b, dt_bias, a_log, d_skip, chunk=128):
    b, t, _ = xbc.shape
    tin = min(chunk, t)
    nc = t // tin
    pad_lanes = lambda v: jnp.concatenate([v.astype(F32), jnp.zeros((LANES - v.shape[0],), F32)])[None, :]
    dtb = pad_lanes(dt_bias)
    a_neg = pad_lanes(-jnp.exp(a_log.astype(F32)))
    dsk = jnp.repeat(d_skip.astype(F32), SSD_HEAD_DIM)[None, :]
    r = lax.broadcasted_iota(jnp.int32, (chunk, chunk), 0)
    c = lax.broadcasted_iota(jnp.int32, (chunk, chunk), 1)
    tril = (c <= r).astype(BF16)
    triu = (r <= c).astype(BF16)
    fixed = lambda bi, ci: (0, 0)
    per_b = lambda bi, ci: (bi, 0, 0)
    y, h_fin = pl.pallas_call(
        functools.partial(_ssd_kernel, chunk=chunk, tin=tin),
        grid=(b, nc),
        in_specs=[pl.BlockSpec((1, tin, CONV_DIM), lambda bi, ci: (bi, ci, 0)),
                  pl.BlockSpec((1, tin, LANES), lambda bi, ci: (bi, ci, 0)),
                  pl.BlockSpec((1, SSD_WIDTH, SSD_STATE), per_b),
                  pl.BlockSpec((1, SUBLANES, CONV_DIM), per_b),
                  pl.BlockSpec((SSD_CONV, CONV_DIM), fixed),
                  pl.BlockSpec((1, CONV_DIM), fixed),
                  pl.BlockSpec((1, LANES), fixed),
                  pl.BlockSpec((1, LANES), fixed),
                  pl.BlockSpec((1, SSD_WIDTH), fixed),
                  pl.BlockSpec((chunk, chunk), fixed),
                  pl.BlockSpec((chunk, chunk), fixed)],
        out_specs=[pl.BlockSpec((1, tin, SSD_WIDTH), lambda bi, ci: (bi, ci, 0)),
                   pl.BlockSpec((1, SSD_WIDTH, SSD_STATE), per_b)],
        out_shape=[jax.ShapeDtypeStruct((b, t, SSD_WIDTH), F32),
                   jax.ShapeDtypeStruct((b, SSD_WIDTH, SSD_STATE), F32)],
        scratch_shapes=[pltpu.VMEM((SUBLANES + chunk, CONV_DIM), F32),
                        pltpu.VMEM((SSD_WIDTH, SSD_STATE), F32)],
        compiler_params=_cparams(("parallel", "arbitrary")),
        name="ssd",
    )(xbc, dt_raw, h0, conv0, conv_w, conv_b[None, :], dtb, a_neg, dsk, tril, triu)
    return y, h_fin


def _mixout_kernel(o_ref, y_ref, z_ref, h_ref, ga_ref, gs_ref, wo_ref, gf_ref, wr_ref, br_ref,
                   h1_ref, f_ref, comb_ref):
    attn = _rms(o_ref[...], ga_ref[...]).astype(BF16)
    z = z_ref[...]
    yg = y_ref[...] * (z * jax.nn.sigmoid(z))
    gw = SSD_WIDTH // SSD_GROUPS
    parts = []
    for g in range(SSD_GROUPS):
        part = yg[:, g * gw:(g + 1) * gw]
        parts.append(part * lax.rsqrt(jnp.mean(part * part, axis=-1, keepdims=True) + EPS))
    yn = (jnp.concatenate(parts, axis=-1) * gs_ref[...]).astype(BF16)
    mix = (jnp.dot(attn, wo_ref[0:SB_WIDTH, :], preferred_element_type=F32)
           + jnp.dot(yn, wo_ref[SB_WIDTH:, :], preferred_element_type=F32))
    h1 = h_ref[...] + mix
    h1_ref[...] = h1
    f = _rms(h1, gf_ref[...])
    f_ref[...] = f.astype(BF16)

    f_hi, f_mid, f_lo = _split3(f)
    w_hi, w_mid, w_lo = wr_ref[0], wr_ref[1], wr_ref[2]
    d = lambda a, b: jnp.dot(a, b, preferred_element_type=F32)
    logits = (d(f_hi, w_hi) + (d(f_hi, w_mid) + d(f_mid, w_hi))
              + (d(f_hi, w_lo) + d(f_mid, w_mid) + d(f_lo, w_hi))) + br_ref[...]

    lane = lax.broadcasted_iota(jnp.int32, logits.shape, 1)
    work = logits
    tops, sels = [], []
    for _ in range(TOP_K):
        m = jnp.max(work, axis=-1, keepdims=True)
        idx = jnp.min(jnp.where(work == m, lane, LANES), axis=-1, keepdims=True)
        sel = lane == idx
        tops.append(m)
        sels.append(sel)
        work = jnp.where(sel, -jnp.inf, work)
    exps = [jnp.exp(t - tops[0]) for t in tops]
    denom = exps[0] + exps[1] + exps[2] + exps[3]
    comb = jnp.zeros(logits.shape, F32)
    for e, sel in zip(exps, sels):
        comb = jnp.where(sel, e / denom, comb)
    comb_ref[...] = comb


def _mixout(o, y, z, h, g_attn, g_ssd, w_out_b, g_ffn, w_router3, b_router_pad, tm=256):
    n = h.shape[0]
    tm = _row_tile(n, tm)
    row = lambda i: (i, 0)
    fixed = lambda i: (0, 0)
    return pl.pallas_call(
        _mixout_kernel,
        grid=(n // tm,),
        in_specs=[pl.BlockSpec((tm, SB_WIDTH), row),
                  pl.BlockSpec((tm, SSD_WIDTH), row),
                  pl.BlockSpec((tm, SSD_WIDTH), row),
                  pl.BlockSpec((tm, D_MODEL), row),
                  pl.BlockSpec((1, SB_WIDTH), fixed),
                  pl.BlockSpec((1, SSD_WIDTH), fixed),
                  pl.BlockSpec((SB_WIDTH + SSD_WIDTH, D_MODEL), fixed),
                  pl.BlockSpec((1, D_MODEL), fixed),
                  pl.BlockSpec((3, D_MODEL, LANES), lambda i: (0, 0, 0)),
                  pl.BlockSpec((1, LANES), fixed)],
        out_specs=[pl.BlockSpec((tm, D_MODEL), row),
                   pl.BlockSpec((tm, D_MODEL), row),
                   pl.BlockSpec((tm, LANES), row)],
        out_shape=[jax.ShapeDtypeStruct((n, D_MODEL), F32),
                   jax.ShapeDtypeStruct((n, D_MODEL), BF16),
                   jax.ShapeDtypeStruct((n, LANES), F32)],
        compiler_params=_cparams(("parallel",)),
        name="mixout",
    )(o, y, z, h, g_attn[None, :], g_ssd[None, :], w_out_b, g_ffn[None, :], w_router3, b_router_pad)


def _swiglu_packed(gu):
    half = D_EXPERT
    lane = lax.broadcasted_iota(jnp.int32, (gu.shape[0], half), 1)
    even = (lane % 2) == 0

    def act(x):
        gate = jnp.minimum(x, SWIGLU_LIMIT)
        up = jnp.clip(pltpu.roll(x, half - 1, 1), -SWIGLU_LIMIT, SWIGLU_LIMIT)
        return (up + 1.0) * (gate * jax.nn.sigmoid(SWIGLU_ALPHA * gate))

    a_lo = act(gu[:, :half])
    a_hi = act(gu[:, half:])
    return jnp.where(even, a_lo, pltpu.roll(a_hi, 1, 1))


def _moe_dense_kernel(f_ref, comb_ref, wgu_ref, bgu_ref, wd_ref, bd_ref, o_ref):
    e = pl.program_id(1)

    @pl.when(e == 0)
    def _():
        o_ref[...] = jnp.zeros_like(o_ref)

    gu = jnp.dot(f_ref[...], wgu_ref[0].astype(BF16), preferred_element_type=F32) + bgu_ref[0]
    act = _swiglu_packed(gu).astype(BF16)
    y = jnp.dot(act, wd_ref[0].astype(BF16), preferred_element_type=F32) + bd_ref[0]
    comb = comb_ref[...]
    lane = lax.broadcasted_iota(jnp.int32, comb.shape, 1)
    gate = jnp.sum(jnp.where(lane == e, comb, 0.0), axis=-1, keepdims=True)
    o_ref[...] += gate * y


def _moe_dense(f, comb, w_gate_up, b_gate_up, w_down_perm, b_down, tm=512):
    n = f.shape[0]
    tm = _row_tile(n, tm)
    row = lambda i, e: (i, 0)
    per_e = lambda i, e: (e, 0, 0)
    return pl.pallas_call(
        _moe_dense_kernel,
        grid=(n // tm, N_EXPERTS),
        in_specs=[pl.BlockSpec((tm, D_MODEL), row),
                  pl.BlockSpec((tm, LANES), row),
                  pl.BlockSpec((1, D_MODEL, 2 * D_EXPERT), per_e),
                  pl.BlockSpec((1, 1, 2 * D_EXPERT), per_e),
                  pl.BlockSpec((1, D_EXPERT, D_MODEL), per_e),
                  pl.BlockSpec((1, 1, D_MODEL), per_e)],
        out_specs=pl.BlockSpec((tm, D_MODEL), row),
        out_shape=jax.ShapeDtypeStruct((n, D_MODEL), F32),
        compiler_params=_cparams(("parallel", "arbitrary")),
        name="moe_dense",
    )(f, comb, w_gate_up, b_gate_up[:, None, :], w_down_perm, b_down[:, None, :])


def _ple_kernel(h1_ref, moe_ref, p_ref, gin_ref, wg_ref, wp_ref, gout_ref, gfin_ref, y_ref, *, final):
    h2 = h1_ref[...] + moe_ref[...]
    c = _rms(h2, gin_ref[...]).astype(BF16)
    gate = jax.nn.sigmoid(jnp.dot(c, wg_ref[...], preferred_element_type=F32))
    emb = _rms(jnp.dot(p_ref[...].astype(BF16), wp_ref[...], preferred_element_type=F32), gout_ref[...])
    h3 = h2 + gate * emb
    y_ref[...] = _rms(h3, gfin_ref[...]) if final else h3


def _ple(h1, moe, row_off, p, g_in, w_gate_b, w_proj_b, g_out, g_final, final, tm=256):
    n = p.shape[0]
    tm = _row_tile(n, tm)
    assert row_off % tm == 0
    off = row_off // tm
    row = lambda i: (i, 0)
    row_o = lambda i: (i + off, 0)
    fixed = lambda i: (0, 0)
    return pl.pallas_call(
        functools.partial(_ple_kernel, final=final),
        grid=(n // tm,),
        in_specs=[pl.BlockSpec((tm, D_MODEL), row_o),
                  pl.BlockSpec((tm, D_MODEL), row_o),
                  pl.BlockSpec((tm, PLE_DIM), row),
                  pl.BlockSpec((1, D_MODEL), fixed),
                  pl.BlockSpec((D_MODEL, D_MODEL), fixed),
                  pl.BlockSpec((PLE_DIM, D_MODEL), fixed),
                  pl.BlockSpec((1, D_MODEL), fixed),
                  pl.BlockSpec((1, D_MODEL), fixed)],
        out_specs=pl.BlockSpec((tm, D_MODEL), row),
        out_shape=jax.ShapeDtypeStruct((n, D_MODEL), F32),
        compiler_params=_cparams(("parallel",)),
        name="ple",
    )(h1, moe, p, g_in[None, :], w_gate_b, w_proj_b, g_out[None, :], g_final[None, :])


def _interleave_halves(w):
    e, d, n = w.shape
    return w.reshape(e, 2, d // 2, n).transpose(0, 2, 1, 3).reshape(e, d, n)


def kernel(x_prompt, x_sample, cache_k, cache_v, state_ssm, state_conv, page_table, p_prompt, p_sample, g_mix, w_in, w_out, g_attn, sb_bias, conv_w, conv_b, dt_bias, a_log, d_skip, g_ssd, g_ffn, w_router, b_router, w_gate_up, b_gate_up, w_down, b_down, g_ple_in, w_ple_gate, w_ple_proj, g_ple_out, g_final):
    depth = w_in.shape[0]
    bp, tp, _ = x_prompt.shape
    bs, ts, _ = x_sample.shape
    n_p, n_s = bp * tp, bs * ts
    h_p = x_prompt.reshape(n_p, D_MODEL)
    h_s = x_sample.reshape(n_s, D_MODEL)
    outs = [[] for _ in range(8)]

    for l in range(depth):
        w_in_pad = jnp.concatenate(
            [w_in[l], jnp.zeros((D_MODEL, _IN_COLS - w_in.shape[2]), w_in.dtype)], axis=1).astype(BF16)
        w_out_b = w_out[l].astype(BF16)
        w_r = jnp.concatenate([w_router[l], jnp.zeros((D_MODEL, LANES - N_EXPERTS), F32)], axis=1)
        w_router3 = jnp.stack(_split3(w_r))
        b_router_pad = jnp.concatenate([b_router[l], jnp.full((LANES - N_EXPERTS,), NEG_BIG, F32)])[None, :]
        w_down_perm = _interleave_halves(w_down[l])
        w_pg = w_ple_gate[l].astype(BF16)
        w_pp = w_ple_proj[l].astype(BF16)

        def mixer_stage(h, b, t, past):
            q, k, v, z, xbc, dt_raw = _inproj(h, g_mix[l][None, :], w_in_pad)
            r3 = lambda a: a.reshape(b, t, a.shape[-1])
            if past is None:
                o = _attn_prompt(r3(q), r3(k), r3(v), sb_bias[l])
                h0 = jnp.zeros((b, SSD_WIDTH, SSD_STATE), F32)
                conv0 = jnp.zeros((b, SUBLANES, CONV_DIM), F32)
            else:
                o = _attn_sample(r3(q), r3(k), r3(v), cache_k[l], cache_v[l], page_table, sb_bias[l])
                h0 = state_ssm[l].reshape(b, SSD_WIDTH, SSD_STATE)
                conv0 = jnp.concatenate(
                    [jnp.zeros((b, SUBLANES - (SSD_CONV - 1), CONV_DIM), F32), state_conv[l]], axis=1)
            y, h_fin = _ssd(r3(xbc), r3(dt_raw), h0, conv0, conv_w[l], conv_b[l], dt_bias[l], a_log[l], d_skip[l])
            conv_new = jnp.concatenate([conv0, r3(xbc)], axis=1)[:, -(SSD_CONV - 1):]
            h1, f, comb = _mixout(o.reshape(b * t, SB_WIDTH), y.reshape(b * t, SSD_WIDTH), z, h,
                                  g_attn[l], g_ssd[l], w_out_b, g_ffn[l], w_router3, b_router_pad)
            k_new = k.reshape(b, t, SB_HEADS, SB_HEAD_DIM)
            v_new = v.reshape(b, t, SB_HEADS, SB_HEAD_DIM)
            ssm_new = h_fin.reshape(b, SSD_HEADS, SSD_HEAD_DIM, SSD_STATE)
            return h1, f, comb, (k_new, v_new, ssm_new, conv_new)

        h1_p, f_p, comb_p, new_p = mixer_stage(h_p, bp, tp, None)
        h1_s, f_s, comb_s, new_s = mixer_stage(h_s, bs, ts, True)
        h1 = jnp.concatenate([h1_p, h1_s], axis=0)
        f = jnp.concatenate([f_p, f_s], axis=0)
        comb = jnp.concatenate([comb_p, comb_s], axis=0)
        moe = _moe_dense(f, comb, w_gate_up[l], b_gate_up[l], w_down_perm, b_down[l])
        last = l == depth - 1
        h_p = _ple(h1, moe, 0, p_prompt[l].reshape(n_p, PLE_DIM), g_ple_in[l], w_pg, w_pp, g_ple_out[l], g_final, last)
        h_s = _ple(h1, moe, n_p, p_sample[l].reshape(n_s, PLE_DIM), g_ple_in[l], w_pg, w_pp, g_ple_out[l], g_final, last)
        for dst, val in zip(outs, new_p + new_s):
            dst.append(val)

    y_prompt = h_p.reshape(bp, tp, D_MODEL)
    y_sample = h_s.reshape(bs, ts, D_MODEL)
    return (y_prompt, y_sample) + tuple(jnp.stack(o) for o in outs)
```

```python
import functools

import jax
import jax.numpy as jnp
from jax import lax
from jax.experimental import pallas as pl
from jax.experimental.pallas import tpu as pltpu

F32 = jnp.float32
BF16 = jnp.bfloat16

D_MODEL = 1024
SB_HEADS = 8
SB_HEAD_DIM = 64
SB_WIDTH = SB_HEADS * SB_HEAD_DIM
SB_SCALE = SB_HEAD_DIM ** -0.5
SSD_HEADS = 8
SSD_HEAD_DIM = 64
SSD_WIDTH = SSD_HEADS * SSD_HEAD_DIM
SSD_GROUPS = 2
SSD_STATE = 128
SSD_CONV = 4
CONV_DIM = SSD_WIDTH + 2 * SSD_GROUPS * SSD_STATE
N_EXPERTS = 32
TOP_K = 4
D_EXPERT = D_MODEL
SWIGLU_LIMIT = 7.0
SWIGLU_ALPHA = 1.702
PLE_DIM = 256
EPS = 1e-6

LANES = 128
SUBLANES = 8
VMEM_LIMIT = 56 * 1024 * 1024
NEG_BIG = -1e30

_DT_PAD = LANES
_IN_CUTS = (SB_WIDTH, SB_WIDTH, SB_WIDTH, SSD_WIDTH, CONV_DIM, _DT_PAD)
_IN_COLS = sum(_IN_CUTS)

_NT = (((1,), (1,)), ((), ()))
_TN = (((0,), (0,)), ((), ()))


def _row_tile(n, tm):
    tm = min(tm, n)
    while n % tm:
        tm -= SUBLANES
    return tm


def _cparams(sem):
    return pltpu.CompilerParams(dimension_semantics=sem, vmem_limit_bytes=VMEM_LIMIT)


def _rms(x, g):
    return x * lax.rsqrt(jnp.mean(x * x, axis=-1, keepdims=True) + EPS) * g


def _split3(x):
    hi = x.astype(BF16)
    r1 = x - hi.astype(F32)
    mid = r1.astype(BF16)
    lo = (r1 - mid.astype(F32)).astype(BF16)
    return hi, mid, lo


def _dot3_l(x, m):
    hi, mid, lo = _split3(x)
    d = lambda a: jnp.dot(a, m, preferred_element_type=F32)
    return d(hi) + d(mid) + d(lo)


def _dot3_r(m, x):
    hi, mid, lo = _split3(x)
    d = lambda a: jnp.dot(m, a, preferred_element_type=F32)
    return d(hi) + d(mid) + d(lo)


def _inproj_kernel(x_ref, g_ref, w_ref, q_ref, k_ref, v_ref, z_ref, xbc_ref, dt_ref):
    a = _rms(x_ref[...], g_ref[...]).astype(BF16)
    c0 = 0
    for ref, width in zip((q_ref, k_ref, v_ref, z_ref, xbc_ref, dt_ref), _IN_CUTS):
        ref[...] = jnp.dot(a, w_ref[:, c0:c0 + width], preferred_element_type=F32)
        c0 += width


def _inproj(x, g, w_pad, tm=256):
    n = x.shape[0]
    tm = _row_tile(n, tm)
    row = lambda i: (i, 0)
    fixed = lambda i: (0, 0)
    return pl.pallas_call(
        _inproj_kernel,
        grid=(n // tm,),
        in_specs=[pl.BlockSpec((tm, D_MODEL), row),
                  pl.BlockSpec((1, D_MODEL), fixed),
                  pl.BlockSpec((D_MODEL, _IN_COLS), fixed)],
        out_specs=[pl.BlockSpec((tm, c), row) for c in _IN_CUTS],
        out_shape=[jax.ShapeDtypeStruct((n, c), F32) for c in _IN_CUTS],
        compiler_params=_cparams(("parallel",)),
        name="inproj",
    )(x, g, w_pad)


def _softplus(z):
    return jnp.maximum(z, 0.0) + jnp.log(1.0 + jnp.exp(-jnp.abs(z)))


def _attn_prompt_kernel(bias_ref, q_ref, k_ref, v_ref, tri_ref, o_ref, acc_ref, car_ref, *, blk):
    hp = pl.program_id(1)
    qi = pl.program_id(2)
    lane = lax.broadcasted_iota(jnp.int32, (blk, LANES), 1)
    first = lane < SB_HEAD_DIM
    q2 = q_ref[0] * SB_SCALE
    qs = jnp.concatenate([jnp.where(first, q2, 0.0), jnp.where(first, 0.0, q2)], axis=0).astype(BF16)
    rows = lax.broadcasted_iota(jnp.int32, (2 * blk, 1), 0)
    bias = jnp.where(rows < blk, bias_ref[hp * 2], bias_ref[hp * 2 + 1])
    tri = tri_ref[...]

    def block(kj, acc, car, causal):
        start = pl.multiple_of(kj * blk, blk)
        kb = k_ref[0, pl.ds(start, blk), :].astype(BF16)
        vb = v_ref[0, pl.ds(start, blk), :].astype(BF16)
        s = lax.dot_general(qs, kb, _NT, preferred_element_type=F32)
        sp = _softplus(s + bias)
        if causal is not None:
            sp = jnp.where(causal, sp, 0.0)
        later = jnp.dot(sp.astype(BF16), tri, preferred_element_type=F32)
        w = jnp.exp((s + (bias - car)) - sp - later)
        if causal is not None:
            w = jnp.where(causal, w, 0.0)
        acc = acc + jnp.dot(w.astype(BF16), vb, preferred_element_type=F32)
        car = car + jnp.sum(sp, axis=-1, keepdims=True)
        return acc, car

    r = lax.broadcasted_iota(jnp.int32, (2 * blk, blk), 0)
    c = lax.broadcasted_iota(jnp.int32, (2 * blk, blk), 1)
    causal = c < jnp.where(r < blk, r, r - blk)
    acc, car = block(qi, jnp.zeros((2 * blk, LANES), F32), jnp.zeros((2 * blk, 1), F32), causal)
    acc_ref[...] = acc
    car_ref[...] = car

    def run(first_block, count):
        acc, car = acc_ref[...], car_ref[...]
        for u in range(count):
            acc, car = block(first_block - u, acc, car, None)
        acc_ref[...] = acc
        car_ref[...] = car

    unroll = 4
    n_main = qi // unroll

    def main(t, carry):
        run(qi - 1 - unroll * t, unroll)
        return carry

    def tail(t, carry):
        run(qi - 1 - unroll * n_main - t, 1)
        return carry

    lax.fori_loop(0, n_main, main, 0)
    lax.fori_loop(0, qi - unroll * n_main, tail, 0)

    o_ref[0] = jnp.where(first, acc_ref[0:blk, :], acc_ref[blk:, :])


def _strict_upper_sum_matrix(n):
    j = lax.broadcasted_iota(jnp.int32, (n, n), 0)
    s = lax.broadcasted_iota(jnp.int32, (n, n), 1)
    return (j > s).astype(BF16)


def _attn_prompt(q, k, v, bias, blk=256):
    b, t, _ = q.shape
    blk = min(blk, t)
    tri = _strict_upper_sum_matrix(blk)
    grid_spec = pltpu.PrefetchScalarGridSpec(
        num_scalar_prefetch=0,
        grid=(b, SB_WIDTH // LANES, t // blk),
        in_specs=[pl.BlockSpec(memory_space=pltpu.SMEM),
                  pl.BlockSpec((1, blk, LANES), lambda bi, h, i: (bi, i, h)),
                  pl.BlockSpec((1, t, LANES), lambda bi, h, i: (bi, 0, h)),
                  pl.BlockSpec((1, t, LANES), lambda bi, h, i: (bi, 0, h)),
                  pl.BlockSpec((blk, blk), lambda bi, h, i: (0, 0))],
        out_specs=pl.BlockSpec((1, blk, LANES), lambda bi, h, i: (bi, i, h)),
        scratch_shapes=[pltpu.VMEM((2 * blk, LANES), F32), pltpu.VMEM((2 * blk, 1), F32)],
    )
    return pl.pallas_call(
        functools.partial(_attn_prompt_kernel, blk=blk),
        grid_spec=grid_spec,
        out_shape=jax.ShapeDtypeStruct((b, t, SB_WIDTH), F32),
        compiler_params=_cparams(("parallel", "parallel", "arbitrary")),
        name="attn_prompt",
    )(bias, q, k, v, tri)


def _attn_sample_kernel(pt_ref, q_ref, kn_ref, vn_ref, bias_ref, trio_ref, *rest, pps, page, tq):
    del pt_ref
    kp_refs = rest[:pps]
    vp_refs = rest[pps:2 * pps]
    o_ref = rest[2 * pps]
    qbd_ref, acc_ref, car_ref = rest[2 * pps + 1:]
    j = pl.program_id(1)
    nj = pl.num_programs(1)
    nq = SB_HEADS * tq

    def attend(kts, vts, mask):
        n = len(kts)
        kcat = jnp.concatenate(kts, axis=1).astype(BF16)
        bias = jnp.concatenate([bias_ref[...]] * n, axis=1)
        z = jnp.dot(qbd_ref[...], kcat, preferred_element_type=F32) + bias
        sp = _softplus(z)
        if mask is not None:
            sp = jnp.where(mask, sp, 0.0)
        sp_rows = jnp.concatenate([sp[:, i * page:(i + 1) * page] for i in range(n)], axis=0)
        hi = sp_rows.astype(BF16)
        mid = (sp_rows - hi.astype(F32)).astype(BF16)
        sums = jnp.dot(jnp.concatenate([hi, mid], axis=1), trio_ref[...], preferred_element_type=F32)
        car = car_ref[...]
        later = []
        for i in range(n):
            part = sums[i * nq:(i + 1) * nq, :]
            later.append(part[:, :page] + car)
            car = car + part[:, page:]
        car_ref[...] = car
        w = jnp.exp(z - sp - jnp.concatenate(later, axis=1))
        if mask is not None:
            w = jnp.where(mask, w, 0.0)
        vcat = jnp.concatenate(vts, axis=1).astype(BF16)
        acc_ref[...] += lax.dot_general(w.astype(BF16), vcat, _NT, preferred_element_type=F32)

    @pl.when(j == 0)
    def _():
        q = q_ref[0] * SB_SCALE
        qt = jnp.concatenate([q] * SB_HEADS, axis=0)
        r = lax.broadcasted_iota(jnp.int32, (nq, SB_WIDTH), 0)
        c = lax.broadcasted_iota(jnp.int32, (nq, SB_WIDTH), 1)
        qbd_ref[...] = jnp.where(r // tq == c // SB_HEAD_DIM, qt, 0.0).astype(BF16)
        acc_ref[...] = jnp.zeros_like(acc_ref)
        car_ref[...] = jnp.zeros_like(car_ref)
        fill = jnp.zeros((page - tq, SB_WIDTH), F32)
        kn_t = jnp.concatenate([kn_ref[0], fill], axis=0).T
        vn_t = jnp.concatenate([vn_ref[0], fill], axis=0).T
        query = lax.broadcasted_iota(jnp.int32, (nq, page), 0) % tq
        key = lax.broadcasted_iota(jnp.int32, (nq, page), 1)
        attend([kn_t], [vn_t], key < query)

    flat = lambda ref: ref[0].reshape(SB_WIDTH, page)
    attend([flat(r) for r in kp_refs], [flat(r) for r in vp_refs], None)

    @pl.when(j == nj - 1)
    def _():
        lane = lax.broadcasted_iota(jnp.int32, (tq, SB_WIDTH), 1)
        out = jnp.zeros((tq, SB_WIDTH), F32)
        for h in range(SB_HEADS):
            rows = acc_ref[h * tq:(h + 1) * tq, :]
            out = jnp.where(lane // SB_HEAD_DIM == h, rows, out)
        o_ref[0] = out


def _attn_sample(q, k_new, v_new, cache_k, cache_v, page_table, bias, pps=8):
    bs, tq, _ = q.shape
    n_pool, page = cache_k.shape[0], cache_k.shape[1]
    n_pages = page_table.shape[1]
    pps = min(pps, n_pages)
    nq = SB_HEADS * tq
    assert n_pages % pps == 0 and tq % SUBLANES == 0
    ck = jnp.transpose(cache_k, (0, 2, 3, 1))
    cv = jnp.transpose(cache_v, (0, 2, 3, 1))
    bias_rows = jnp.broadcast_to(jnp.repeat(bias, tq)[:, None], (nq, page))
    jj = lax.broadcasted_iota(jnp.int32, (page, page), 0)
    ss = lax.broadcasted_iota(jnp.int32, (page, page), 1)
    half = jnp.concatenate([(jj > ss).astype(BF16), jnp.ones((page, page), BF16)], axis=1)
    trio = jnp.concatenate([half, half], axis=0)

    def page_spec(i):
        return pl.BlockSpec((1, SB_HEADS, SB_HEAD_DIM, page),
                            lambda b, j, pt: (pt[b, n_pages - 1 - (j * pps + i)], 0, 0, 0))

    per_seq = pl.BlockSpec((1, tq, SB_WIDTH), lambda b, j, pt: (b, 0, 0))
    grid_spec = pltpu.PrefetchScalarGridSpec(
        num_scalar_prefetch=1,
        grid=(bs, n_pages // pps),
        in_specs=[per_seq, per_seq, per_seq,
                  pl.BlockSpec((nq, page), lambda b, j, pt: (0, 0)),
                  pl.BlockSpec((2 * page, 2 * page), lambda b, j, pt: (0, 0))]
                 + [page_spec(i) for i in range(pps)] * 2,
        out_specs=per_seq,
        scratch_shapes=[pltpu.VMEM((nq, SB_WIDTH), BF16),
                        pltpu.VMEM((nq, SB_WIDTH), F32),
                        pltpu.VMEM((nq, page), F32)],
    )
    return pl.pallas_call(
        functools.partial(_attn_sample_kernel, pps=pps, page=page, tq=tq),
        grid_spec=grid_spec,
        out_shape=jax.ShapeDtypeStruct((bs, tq, SB_WIDTH), F32),
        compiler_params=_cparams(("parallel", "arbitrary")),
        name="attn_sample",
    )(page_table, q, k_new, v_new, bias_rows, trio, *([ck] * pps), *([cv] * pps))


def _ssd_kernel(xbc_ref, dt_ref, h0_ref, c0_ref, cw_ref, cb_ref, dtb_ref, a_ref, dsk_ref,
                tril_ref, triu_ref, y_ref, hout_ref, xext_ref, h_ref, *, chunk, tin):
    ci = pl.program_id(1)
    nc = pl.num_programs(1)
    pad = SUBLANES

    @pl.when(ci == 0)
    def _():
        h_ref[...] = h0_ref[0]
        xext_ref[0:pad, :] = c0_ref[0]

    @pl.when(ci > 0)
    def _():
        xext_ref[0:pad, :] = xext_ref[chunk:chunk + pad, :]

    if tin < chunk:
        xext_ref[pad:, :] = jnp.zeros((chunk, CONV_DIM), F32)
    xext_ref[pad:pad + tin, :] = xbc_ref[0]

    conv = cb_ref[...] + sum(
        xext_ref[pad - (SSD_CONV - 1) + jj: pad - (SSD_CONV - 1) + jj + chunk, :] * cw_ref[jj:jj + 1, :]
        for jj in range(SSD_CONV))
    u = conv * jax.nn.sigmoid(conv)

    if tin < chunk:
        dt_in = jnp.concatenate([dt_ref[0], jnp.zeros((chunk - tin, LANES), F32)], axis=0)
    else:
        dt_in = dt_ref[0]
    dt = jax.nn.softplus(dt_in + dtb_ref[...])
    if tin < chunk:
        live = lax.broadcasted_iota(jnp.int32, (chunk, LANES), 0) < tin
        dt = jnp.where(live, dt, 0.0)
    a = dt * a_ref[...]
    acs = _dot3_r(tril_ref[...], a)
    acs_t = _dot3_l(a.T, triu_ref[...])
    acs_last = acs[chunk - 1:chunk, :]
    decay_in = jnp.exp(acs_last - acs)
    e_acs = jnp.exp(acs)
    cdec_t = jnp.exp(acs_t[:, chunk - 1:chunk])

    lrow = lax.broadcasted_iota(jnp.int32, (chunk, chunk), 0)
    scol = lax.broadcasted_iota(jnp.int32, (chunk, chunk), 1)
    causal = scol <= lrow
    lane = lax.broadcasted_iota(jnp.int32, (chunk, LANES), 1)
    first = lane < SSD_HEAD_DIM
    srow = lax.broadcasted_iota(jnp.int32, (LANES, LANES), 0)

    hpg = SSD_HEADS // SSD_GROUPS
    for g in range(SSD_GROUPS):
        b0 = SSD_WIDTH + g * SSD_STATE
        c0 = SSD_WIDTH + SSD_GROUPS * SSD_STATE + g * SSD_STATE
        bm = u[:, b0:b0 + SSD_STATE].astype(BF16)
        cm = u[:, c0:c0 + SSD_STATE].astype(BF16)
        cb = lax.dot_general(cm, bm, _NT, preferred_element_type=F32)
        for pr in range(hpg // 2):
            h0i = g * hpg + 2 * pr
            h1i = h0i + 1
            x2 = u[:, h0i * SSD_HEAD_DIM:(h0i + 2) * SSD_HEAD_DIM]
            pick = lambda m: jnp.where(first, m[:, h0i:h0i + 1], m[:, h1i:h1i + 1])
            xdt = x2 * pick(dt)
            xdt_b = xdt.astype(BF16)
            ys = []
            for hi in (h0i, h1i):
                seg = acs[:, hi:hi + 1] - acs_t[hi:hi + 1, :]
                lmat = jnp.exp(jnp.where(causal, seg, -jnp.inf))
                ys.append(jnp.dot((cb * lmat).astype(BF16), xdt_b, preferred_element_type=F32))
            y_diag = jnp.where(first, ys[0], ys[1])
            hrows = slice(h0i * SSD_HEAD_DIM, (h0i + 2) * SSD_HEAD_DIM)
            h_prev = h_ref[hrows, :]
            y_off = lax.dot_general(cm, h_prev.astype(BF16), _NT, preferred_element_type=F32) * pick(e_acs)
            states = lax.dot_general((xdt * pick(decay_in)).astype(BF16), bm, _TN,
                                     preferred_element_type=F32)
            cdec = jnp.where(srow < SSD_HEAD_DIM, cdec_t[h0i:h0i + 1, :], cdec_t[h1i:h1i + 1, :])
            h_ref[hrows, :] = cdec * h_prev + states
            y2 = y_diag + y_off + dsk_ref[:, h0i * SSD_HEAD_DIM:(h0i + 2) * SSD_HEAD_DIM] * x2
            y_ref[0, :, h0i * SSD_HEAD_DIM:(h0i + 2) * SSD_HEAD_DIM] = y2[0:tin, :]

    @pl.when(ci == nc - 1)
    def _():
        hout_ref[0] = h_ref[...]


def _ssd(xbc, dt_raw, h0, conv0, conv_w, conv_b, dt_bias, a_log, d_skip, chunk=128):
    b, t, _ = xbc.shape
    tin = min(chunk, t)
    nc = t // tin
    pad_lanes = lambda v: jnp.concatenate([v.astype(F32), jnp.zeros((LANES - v.shape[0],), F32)])[None, :]
    dtb = pad_lanes(dt_bias)
    a_neg = pad_lanes(-jnp.exp(a_log.astype(F32)))
    dsk = jnp.repeat(d_skip.astype(F32), SSD_HEAD_DIM)[None, :]
    r = lax.broadcasted_iota(jnp.int32, (chunk, chunk), 0)
    c = lax.broadcasted_iota(jnp.int32, (chunk, chunk), 1)
    tril = (c <= r).astype(BF16)
    triu = (r <= c).astype(BF16)
    fixed = lambda bi, ci: (0, 0)
    per_b = lambda bi, ci: (bi, 0, 0)
    y, h_fin = pl.pallas_call(
        functools.partial(_ssd_kernel, chunk=chunk, tin=tin),
        grid=(b, nc),
        in_specs=[pl.BlockSpec((1, tin, CONV_DIM), lambda bi, ci: (bi, ci, 0)),
                  pl.BlockSpec((1, tin, LANES), lambda bi, ci: (bi, ci, 0)),
                  pl.BlockSpec((1, SSD_WIDTH, SSD_STATE), per_b),
                  pl.BlockSpec((1, SUBLANES, CONV_DIM), per_b),
                  pl.BlockSpec((SSD_CONV, CONV_DIM), fixed),
                  pl.BlockSpec((1, CONV_DIM), fixed),
                  pl.BlockSpec((1, LANES), fixed),
                  pl.BlockSpec((1, LANES), fixed),
                  pl.BlockSpec((1, SSD_WIDTH), fixed),
                  pl.BlockSpec((chunk, chunk), fixed),
                  pl.BlockSpec((chunk, chunk), fixed)],
        out_specs=[pl.BlockSpec((1, tin, SSD_WIDTH), lambda bi, ci: (bi, ci, 0)),
                   pl.BlockSpec((1, SSD_WIDTH, SSD_STATE), per_b)],
        out_shape=[jax.ShapeDtypeStruct((b, t, SSD_WIDTH), F32),
                   jax.ShapeDtypeStruct((b, SSD_WIDTH, SSD_STATE), F32)],
        scratch_shapes=[pltpu.VMEM((SUBLANES + chunk, CONV_DIM), F32),
                        pltpu.VMEM((SSD_WIDTH, SSD_STATE), F32)],
        compiler_params=_cparams(("parallel", "arbitrary")),
        name="ssd",
    )(xbc, dt_raw, h0, conv0, conv_w, conv_b[None, :], dtb, a_neg, dsk, tril, triu)
    return y, h_fin


def _mixout_kernel(o_ref, y_ref, z_ref, h_ref, ga_ref, gs_ref, wo_ref, gf_ref, wr_ref, br_ref,
                   h1_ref, f_ref, logit_ref):
    attn = _rms(o_ref[...], ga_ref[...]).astype(BF16)
    z = z_ref[...]
    yg = y_ref[...] * (z * jax.nn.sigmoid(z))
    gw = SSD_WIDTH // SSD_GROUPS
    parts = []
    for g in range(SSD_GROUPS):
        part = yg[:, g * gw:(g + 1) * gw]
        parts.append(part * lax.rsqrt(jnp.mean(part * part, axis=-1, keepdims=True) + EPS))
    yn = (jnp.concatenate(parts, axis=-1) * gs_ref[...]).astype(BF16)
    mix = (jnp.dot(attn, wo_ref[0:SB_WIDTH, :], preferred_element_type=F32)
           + jnp.dot(yn, wo_ref[SB_WIDTH:, :], preferred_element_type=F32))
    h1 = h_ref[...] + mix
    h1_ref[...] = h1
    f = _rms(h1, gf_ref[...])
    f_ref[...] = f.astype(BF16)

    f_hi, f_mid, f_lo = _split3(f)
    w_hi, w_mid, w_lo = wr_ref[0], wr_ref[1], wr_ref[2]
    d = lambda a, b: jnp.dot(a, b, preferred_element_type=F32)
    logit_ref[...] = (d(f_hi, w_hi) + (d(f_hi, w_mid) + d(f_mid, w_hi))
                      + (d(f_hi, w_lo) + d(f_mid, w_mid) + d(f_lo, w_hi))) + br_ref[...]


def _mixout(o, y, z, h, g_attn, g_ssd, w_out_b, g_ffn, w_router3, b_router_pad, tm=256):
    n = h.shape[0]
    tm = _row_tile(n, tm)
    row = lambda i: (i, 0)
    fixed = lambda i: (0, 0)
    return pl.pallas_call(
        _mixout_kernel,
        grid=(n // tm,),
        in_specs=[pl.BlockSpec((tm, SB_WIDTH), row),
                  pl.BlockSpec((tm, SSD_WIDTH), row),
                  pl.BlockSpec((tm, SSD_WIDTH), row),
                  pl.BlockSpec((tm, D_MODEL), row),
                  pl.BlockSpec((1, SB_WIDTH), fixed),
                  pl.BlockSpec((1, SSD_WIDTH), fixed),
                  pl.BlockSpec((SB_WIDTH + SSD_WIDTH, D_MODEL), fixed),
                  pl.BlockSpec((1, D_MODEL), fixed),
                  pl.BlockSpec((3, D_MODEL, LANES), lambda i: (0, 0, 0)),
                  pl.BlockSpec((1, LANES), fixed)],
        out_specs=[pl.BlockSpec((tm, D_MODEL), row),
                   pl.BlockSpec((tm, D_MODEL), row),
                   pl.BlockSpec((tm, LANES), row)],
        out_shape=[jax.ShapeDtypeStruct((n, D_MODEL), F32),
                   jax.ShapeDtypeStruct((n, D_MODEL), BF16),
                   jax.ShapeDtypeStruct((n, LANES), F32)],
        compiler_params=_cparams(("parallel",)),
        name="mixout",
    )(o, y, z, h, g_attn[None, :], g_ssd[None, :], w_out_b, g_ffn[None, :], w_router3, b_router_pad)


def _swiglu_packed(gu):
    half = gu.shape[1] // 2
    lane = lax.broadcasted_iota(jnp.int32, (gu.shape[0], half), 1)
    even = (lane % 2) == 0

    def act(x):
        gate = jnp.minimum(x, SWIGLU_LIMIT)
        up = jnp.clip(pltpu.roll(x, half - 1, 1), -SWIGLU_LIMIT, SWIGLU_LIMIT)
        return (up + 1.0) * (gate * jax.nn.sigmoid(SWIGLU_ALPHA * gate))

    a_lo = act(gu[:, :half])
    a_hi = act(gu[:, half:])
    return jnp.where(even, a_lo, pltpu.roll(a_hi, 1, 1))


MOE_TILE = 512
MOE_PIECE = 16
MOE_ROWS = 256


def _moe_pieces(tile):
    return tile * TOP_K // MOE_PIECE + N_EXPERTS


def _route_kernel(lg_ref, below_ref, before_ref, pos_ref, gate_ref, post_ref, cnt_ref):
    logits = lg_ref[...]
    tile = logits.shape[0]
    lane = lax.broadcasted_iota(jnp.int32, logits.shape, 1)
    work = logits
    tops, sels = [], []
    for _ in range(TOP_K):
        m = jnp.max(work, axis=-1, keepdims=True)
        idx = jnp.min(jnp.where(work == m, lane, LANES), axis=-1, keepdims=True)
        sel = lane == idx
        tops.append(m)
        sels.append(sel)
        work = jnp.where(sel, -jnp.inf, work)
    exps = [jnp.exp(t - tops[0]) for t in tops]
    denom = exps[0] + exps[1] + exps[2] + exps[3]

    member = jnp.zeros(logits.shape, F32)
    for sel in sels:
        member = jnp.where(sel, 1.0, member)
    rank = jnp.dot(below_ref[...], member.astype(BF16), preferred_element_type=F32)
    cnt = jnp.sum(member, axis=0, keepdims=True)
    pieces = jnp.floor((cnt + (MOE_PIECE - 1)) * (1.0 / MOE_PIECE))
    pieces8 = jnp.broadcast_to(pieces, (SUBLANES, LANES))
    start = jnp.dot(pieces8.astype(BF16), before_ref[...], preferred_element_type=F32)[0:1, :] * MOE_PIECE
    slot = start + rank
    pos = jnp.full(logits.shape, -1.0, F32)
    gate = jnp.zeros(logits.shape, F32)
    for k in range(TOP_K):
        pk = jnp.sum(jnp.where(sels[k], slot, 0.0), axis=-1, keepdims=True)
        pos = jnp.where(lane == k, pk, pos)
        gate = jnp.where(lane == k, exps[k] / denom, gate)
    pos_ref[...] = pos
    gate_ref[...] = gate
    post_ref[0] = pos.T[0:SUBLANES, :]
    cnt_ref[0] = pieces8


def _route(logits, tile):
    n = logits.shape[0]
    nt = n // tile
    r = lax.broadcasted_iota(jnp.int32, (tile, tile), 0)
    c = lax.broadcasted_iota(jnp.int32, (tile, tile), 1)
    below = (c < r).astype(BF16)
    e0 = lax.broadcasted_iota(jnp.int32, (LANES, LANES), 0)
    e1 = lax.broadcasted_iota(jnp.int32, (LANES, LANES), 1)
    before = (e0 < e1).astype(BF16)
    row = lambda i: (i, 0)
    fixed = lambda i: (0, 0)
    return pl.pallas_call(
        _route_kernel,
        grid=(nt,),
        in_specs=[pl.BlockSpec((tile, LANES), row),
                  pl.BlockSpec((tile, tile), fixed),
                  pl.BlockSpec((LANES, LANES), fixed)],
        out_specs=[pl.BlockSpec((tile, LANES), row),
                   pl.BlockSpec((tile, LANES), row),
                   pl.BlockSpec((1, SUBLANES, tile), lambda i: (i, 0, 0)),
                   pl.BlockSpec((1, SUBLANES, LANES), lambda i: (i, 0, 0))],
        out_shape=[jax.ShapeDtypeStruct((n, LANES), F32),
                   jax.ShapeDtypeStruct((n, LANES), F32),
                   jax.ShapeDtypeStruct((nt, SUBLANES, tile), F32),
                   jax.ShapeDtypeStruct((nt, SUBLANES, LANES), F32)],
        compiler_params=_cparams(("parallel",)),
        name="moe_route",
    )(logits, below, before)


def _piece_copy(src, src_row, dst, dst_row, sem):
    return pltpu.make_async_copy(src.at[pl.ds(pl.multiple_of(src_row, MOE_PIECE), MOE_PIECE)],
                                 dst.at[pl.ds(pl.multiple_of(dst_row, MOE_PIECE), MOE_PIECE)], sem)


def _moe_gather_kernel(dest_ref, np_ref, f_ref, post_ref, xs_in_ref, xs_ref, xl_ref, sem, *, pmax):
    del xs_in_ref
    i = pl.program_id(0)
    tile = f_ref.shape[0]
    rows = lax.broadcasted_iota(jnp.int32, (pmax * MOE_PIECE, tile), 0).astype(F32)
    post = post_ref[0]
    onehot = jnp.zeros(rows.shape, F32)
    for k in range(TOP_K):
        onehot = onehot + jnp.where(rows == post[k:k + 1, :], 1.0, 0.0)
    sorted_rows = jnp.dot(onehot.astype(BF16), f_ref[...], preferred_element_type=F32).astype(BF16)

    def copy(tile_idx, p):
        return _piece_copy(xl_ref, p * MOE_PIECE, xs_ref, dest_ref[tile_idx * pmax + p], sem)

    def wait_tile(tile_idx):
        def wait(p, c):
            copy(tile_idx, p).wait()
            return c
        lax.fori_loop(0, np_ref[tile_idx], wait, 0)

    @pl.when(i > 0)
    def _():
        wait_tile(i - 1)

    xl_ref[...] = sorted_rows

    def start(p, c):
        copy(i, p).start()
        return c

    lax.fori_loop(0, np_ref[i], start, 0)

    @pl.when(i == pl.num_programs(0) - 1)
    def _():
        wait_tile(i)


def _moe_gather(f, post, dest, npieces, total_rows, tile):
    n = f.shape[0]
    pmax = _moe_pieces(tile)
    xs0 = jnp.zeros((total_rows, D_MODEL), BF16)
    grid_spec = pltpu.PrefetchScalarGridSpec(
        num_scalar_prefetch=2,
        grid=(n // tile,),
        in_specs=[pl.BlockSpec((tile, D_MODEL), lambda i, d, c: (i, 0)),
                  pl.BlockSpec((1, SUBLANES, tile), lambda i, d, c: (i, 0, 0)),
                  pl.BlockSpec(memory_space=pl.ANY)],
        out_specs=pl.BlockSpec(memory_space=pl.ANY),
        scratch_shapes=[pltpu.VMEM((pmax * MOE_PIECE, D_MODEL), BF16), pltpu.SemaphoreType.DMA],
    )
    return pl.pallas_call(
        functools.partial(_moe_gather_kernel, pmax=pmax),
        grid_spec=grid_spec,
        out_shape=jax.ShapeDtypeStruct((total_rows, D_MODEL), BF16),
        input_output_aliases={4: 0},
        compiler_params=_cparams(("arbitrary",)),
        name="moe_gather",
    )(dest, npieces, f, post, xs0)


def _moe_expert_kernel(te_ref, tv_ref, xs_ref, wgu_ref, bgu_ref, wd_ref, bd_ref, ys_ref,
                       wgu_b, wd_f, wd_b):
    t = pl.program_id(0)
    e = te_ref[t]
    prev = te_ref[jnp.maximum(t - 1, 0)]

    @pl.when((t == 0) | (e != prev))
    def _():
        wgu_b[...] = wgu_ref[0].astype(BF16)
        half = D_EXPERT // 2
        for c in range(D_MODEL // LANES):
            cols = slice(c * LANES, (c + 1) * LANES)
            wd_f[c, pl.ds(0, half, stride=2), :] = wd_ref[0, 0:half, cols]
            wd_f[c, pl.ds(1, half, stride=2), :] = wd_ref[0, half:, cols]
            wd_b[:, cols] = wd_f[c].astype(BF16)

    @pl.when(tv_ref[t] == 1)
    def _():
        gu = jnp.dot(xs_ref[...], wgu_b[...], preferred_element_type=F32) + bgu_ref[0]
        act = _swiglu_packed(gu).astype(BF16)
        y = jnp.dot(act, wd_b[...], preferred_element_type=F32) + bd_ref[0]
        ys_ref[...] = y.astype(BF16)

    @pl.when(tv_ref[t] == 0)
    def _():
        ys_ref[...] = jnp.zeros_like(ys_ref)


def _moe_experts(xs, tile_expert, tile_valid, w_gate_up, b_gate_up, w_down, b_down):
    rows = xs.shape[0]
    per_e = lambda t, te, tv: (te[t], 0, 0)
    row = lambda t, te, tv: (t, 0)
    grid_spec = pltpu.PrefetchScalarGridSpec(
        num_scalar_prefetch=2,
        grid=(rows // MOE_ROWS,),
        in_specs=[pl.BlockSpec((MOE_ROWS, D_MODEL), row),
                  pl.BlockSpec((1, D_MODEL, 2 * D_EXPERT), per_e),
                  pl.BlockSpec((1, 1, 2 * D_EXPERT), per_e),
                  pl.BlockSpec((1, D_EXPERT, D_MODEL), per_e),
                  pl.BlockSpec((1, 1, D_MODEL), per_e)],
        out_specs=pl.BlockSpec((MOE_ROWS, D_MODEL), row),
        scratch_shapes=[pltpu.VMEM((D_MODEL, 2 * D_EXPERT), BF16),
                        pltpu.VMEM((D_MODEL // LANES, D_EXPERT, LANES), F32),
                        pltpu.VMEM((D_EXPERT, D_MODEL), BF16)],
    )
    return pl.pallas_call(
        _moe_expert_kernel,
        grid_spec=grid_spec,
        out_shape=jax.ShapeDtypeStruct((rows, D_MODEL), BF16),
        compiler_params=_cparams(("arbitrary",)),
        name="moe_experts",
    )(tile_expert, tile_valid, xs, w_gate_up, b_gate_up[:, None, :], w_down, b_down[:, None, :])


def _moe_combine_kernel(dest_ref, np_ref, pos_ref, gate_ref, ys_ref, o_ref, yl_ref, sem, *, pmax):
    i = pl.program_id(0)
    slot = i % 2

    def copy(tile_idx, p):
        s = tile_idx % 2
        return _piece_copy(ys_ref, dest_ref[tile_idx * pmax + p], yl_ref.at[s], p * MOE_PIECE, sem.at[s])

    def start_tile(tile_idx):
        def start(p, c):
            copy(tile_idx, p).start()
            return c
        lax.fori_loop(0, np_ref[tile_idx], start, 0)

    @pl.when(i == 0)
    def _():
        yl_ref[...] = jnp.zeros_like(yl_ref)
        start_tile(0)

    @pl.when(i + 1 < pl.num_programs(0))
    def _():
        start_tile(i + 1)

    def wait(p, c):
        copy(i, p).wait()
        return c

    lax.fori_loop(0, np_ref[i], wait, 0)

    tile = pos_ref.shape[0]
    cols = lax.broadcasted_iota(jnp.int32, (tile, pmax * MOE_PIECE), 1).astype(F32)
    pos = pos_ref[...]
    gate = gate_ref[...]
    weights = jnp.zeros(cols.shape, F32)
    for k in range(TOP_K):
        weights = weights + jnp.where(cols == pos[:, k:k + 1], gate[:, k:k + 1], 0.0)
    o_ref[...] = jnp.dot(weights.astype(BF16), yl_ref[slot], preferred_element_type=F32)


def _moe_combine(ys, pos, gate, dest, npieces, tile):
    n = pos.shape[0]
    pmax = _moe_pieces(tile)
    grid_spec = pltpu.PrefetchScalarGridSpec(
        num_scalar_prefetch=2,
        grid=(n // tile,),
        in_specs=[pl.BlockSpec((tile, LANES), lambda i, d, c: (i, 0)),
                  pl.BlockSpec((tile, LANES), lambda i, d, c: (i, 0)),
                  pl.BlockSpec(memory_space=pl.ANY)],
        out_specs=pl.BlockSpec((tile, D_MODEL), lambda i, d, c: (i, 0)),
        scratch_shapes=[pltpu.VMEM((2, pmax * MOE_PIECE, D_MODEL), BF16), pltpu.SemaphoreType.DMA((2,))],
    )
    return pl.pallas_call(
        functools.partial(_moe_combine_kernel, pmax=pmax),
        grid_spec=grid_spec,
        out_shape=jax.ShapeDtypeStruct((n, D_MODEL), F32),
        compiler_params=_cparams(("arbitrary",)),
        name="moe_combine",
    )(dest, npieces, pos, gate, ys)


def _moe_tile(n):
    tile = min(MOE_TILE, n)
    while n % tile or tile % MOE_PIECE:
        tile -= SUBLANES
    return tile


def _moe(f, logits, w_gate_up, b_gate_up, w_down, b_down):
    n = f.shape[0]
    tile = _moe_tile(n)
    nt = n // tile
    pmax = _moe_pieces(tile)
    pos, gate, post, cnt = _route(logits, tile)

    i32 = jnp.int32
    rows = cnt[:, 0, :N_EXPERTS].astype(i32) * MOE_PIECE
    lstart = jnp.cumsum(rows, axis=1) - rows
    group = jnp.sum(rows, axis=0)
    group_pad = (group + MOE_ROWS - 1) // MOE_ROWS * MOE_ROWS
    goff = jnp.cumsum(group_pad) - group_pad
    gbase = goff[None, :] + jnp.cumsum(rows, axis=0) - rows
    npieces = (jnp.sum(rows, axis=1) // MOE_PIECE).astype(i32)
    p_row = jnp.arange(pmax, dtype=i32) * MOE_PIECE
    p3 = p_row[None, :, None]
    owns = (p3 >= lstart[:, None, :]) & (p3 < (lstart + rows)[:, None, :])
    dest = jnp.sum(jnp.where(owns, (gbase - lstart)[:, None, :] + p3, 0), axis=-1).astype(i32).reshape(-1)

    n_tiles = (n * TOP_K + nt * N_EXPERTS * (MOE_PIECE - 1)) // MOE_ROWS + N_EXPERTS + 1
    total_rows = n_tiles * MOE_ROWS
    t_row = jnp.arange(n_tiles, dtype=i32) * MOE_ROWS
    gend = goff + group_pad
    tile_expert = jnp.minimum(jnp.sum(t_row[:, None] >= gend[None, :], axis=-1), N_EXPERTS - 1).astype(i32)
    tile_valid = (t_row < gend[-1]).astype(i32)

    xs = _moe_gather(f, post, dest, npieces, total_rows, tile)
    ys = _moe_experts(xs, tile_expert, tile_valid, w_gate_up, b_gate_up, w_down, b_down)
    return _moe_combine(ys, pos, gate, dest, npieces, tile)


def _ple_kernel(h1_ref, moe_ref, p_ref, gin_ref, wg_ref, wp_ref, gout_ref, gfin_ref, y_ref, *, final):
    h2 = h1_ref[...] + moe_ref[...]
    c = _rms(h2, gin_ref[...]).astype(BF16)
    gate = jax.nn.sigmoid(jnp.dot(c, wg_ref[...], preferred_element_type=F32))
    emb = _rms(jnp.dot(p_ref[...].astype(BF16), wp_ref[...], preferred_element_type=F32), gout_ref[...])
    h3 = h2 + gate * emb
    y_ref[...] = _rms(h3, gfin_ref[...]) if final else h3


def _ple(h1, moe, row_off, p, g_in, w_gate_b, w_proj_b, g_out, g_final, final, tm=256):
    n = p.shape[0]
    tm = _row_tile(n, tm)
    assert row_off % tm == 0
    off = row_off // tm
    row = lambda i: (i, 0)
    row_o = lambda i: (i + off, 0)
    fixed = lambda i: (0, 0)
    return pl.pallas_call(
        functools.partial(_ple_kernel, final=final),
        grid=(n // tm,),
        in_specs=[pl.BlockSpec((tm, D_MODEL), row),
                  pl.BlockSpec((tm, D_MODEL), row_o),
                  pl.BlockSpec((tm, PLE_DIM), row),
                  pl.BlockSpec((1, D_MODEL), fixed),
                  pl.BlockSpec((D_MODEL, D_MODEL), fixed),
                  pl.BlockSpec((PLE_DIM, D_MODEL), fixed),
                  pl.BlockSpec((1, D_MODEL), fixed),
                  pl.BlockSpec((1, D_MODEL), fixed)],
        out_specs=pl.BlockSpec((tm, D_MODEL), row),
        out_shape=jax.ShapeDtypeStruct((n, D_MODEL), F32),
        compiler_params=_cparams(("parallel",)),
        name="ple",
    )(h1, moe, p, g_in[None, :], w_gate_b, w_proj_b, g_out[None, :], g_final[None, :])


def kernel(x_prompt, x_sample, cache_k, cache_v, state_ssm, state_conv, page_table, p_prompt, p_sample, g_mix, w_in, w_out, g_attn, sb_bias, conv_w, conv_b, dt_bias, a_log, d_skip, g_ssd, g_ffn, w_router, b_router, w_gate_up, b_gate_up, w_down, b_down, g_ple_in, w_ple_gate, w_ple_proj, g_ple_out, g_final):
    depth = w_in.shape[0]
    bp, tp, _ = x_prompt.shape
    bs, ts, _ = x_sample.shape
    n_p, n_s = bp * tp, bs * ts
    h_p = x_prompt.reshape(n_p, D_MODEL)
    h_s = x_sample.reshape(n_s, D_MODEL)
    outs = [[] for _ in range(8)]

    for l in range(depth):
        w_in_pad = jnp.concatenate(
            [w_in[l], jnp.zeros((D_MODEL, _IN_COLS - w_in.shape[2]), w_in.dtype)], axis=1).astype(BF16)
        w_out_b = w_out[l].astype(BF16)
        w_r = jnp.concatenate([w_router[l], jnp.zeros((D_MODEL, LANES - N_EXPERTS), F32)], axis=1)
        w_router3 = jnp.stack(_split3(w_r))
        b_router_pad = jnp.concatenate([b_router[l], jnp.full((LANES - N_EXPERTS,), NEG_BIG, F32)])[None, :]
        w_pg = w_ple_gate[l].astype(BF16)
        w_pp = w_ple_proj[l].astype(BF16)

        def mixer_stage(h, b, t, past):
            q, k, v, z, xbc, dt_raw = _inproj(h, g_mix[l][None, :], w_in_pad)
            r3 = lambda a: a.reshape(b, t, a.shape[-1])
            if past is None:
                o = _attn_prompt(r3(q), r3(k), r3(v), sb_bias[l])
                h0 = jnp.zeros((b, SSD_WIDTH, SSD_STATE), F32)
                conv0 = jnp.zeros((b, SUBLANES, CONV_DIM), F32)
            else:
                o = _attn_sample(r3(q), r3(k), r3(v), cache_k[l], cache_v[l], page_table, sb_bias[l])
                h0 = state_ssm[l].reshape(b, SSD_WIDTH, SSD_STATE)
                conv0 = jnp.concatenate(
                    [jnp.zeros((b, SUBLANES - (SSD_CONV - 1), CONV_DIM), F32), state_conv[l]], axis=1)
            y, h_fin = _ssd(r3(xbc), r3(dt_raw), h0, conv0, conv_w[l], conv_b[l], dt_bias[l], a_log[l], d_skip[l])
            conv_new = jnp.concatenate([conv0, r3(xbc)], axis=1)[:, -(SSD_CONV - 1):]
            h1, f, logits = _mixout(o.reshape(b * t, SB_WIDTH), y.reshape(b * t, SSD_WIDTH), z, h,
                                    g_attn[l], g_ssd[l], w_out_b, g_ffn[l], w_router3, b_router_pad)
            k_new = k.reshape(b, t, SB_HEADS, SB_HEAD_DIM)
            v_new = v.reshape(b, t, SB_HEADS, SB_HEAD_DIM)
            ssm_new = h_fin.reshape(b, SSD_HEADS, SSD_HEAD_DIM, SSD_STATE)
            return h1, f, logits, (k_new, v_new, ssm_new, conv_new)

        h1_p, f_p, lg_p, new_p = mixer_stage(h_p, bp, tp, None)
        h1_s, f_s, lg_s, new_s = mixer_stage(h_s, bs, ts, True)
        f = jnp.concatenate([f_p, f_s], axis=0)
        logits = jnp.concatenate([lg_p, lg_s], axis=0)
        moe = _moe(f, logits, w_gate_up[l], b_gate_up[l], w_down[l], b_down[l])
        last = l == depth - 1
        h_p = _ple(h1_p, moe, 0, p_prompt[l].reshape(n_p, PLE_DIM), g_ple_in[l], w_pg, w_pp, g_ple_out[l], g_final, last)
        h_s = _ple(h1_s, moe, n_p, p_sample[l].reshape(n_s, PLE_DIM), g_ple_in[l], w_pg, w_pp, g_ple_out[l], g_final, last)
        for dst, val in zip(outs, new_p + new_s):
            dst.append(val)

    y_prompt = h_p.reshape(bp, tp, D_MODEL)
    y_sample = h_s.reshape(bs, ts, D_MODEL)
    return (y_prompt, y_sample) + tuple(jnp.stack(o) for o in outs)
```

```python
import functools

import jax
import jax.numpy as jnp
from jax import lax
from jax.experimental import pallas as pl
from jax.experimental.pallas import tpu as pltpu

F32 = jnp.float32
BF16 = jnp.bfloat16

D_MODEL = 1024
SB_HEADS = 8
SB_HEAD_DIM = 64
SB_WIDTH = SB_HEADS * SB_HEAD_DIM
SB_SCALE = SB_HEAD_DIM ** -0.5
SSD_HEADS = 8
SSD_HEAD_DIM = 64
SSD_WIDTH = SSD_HEADS * SSD_HEAD_DIM
SSD_GROUPS = 2
SSD_STATE = 128
SSD_CONV = 4
CONV_DIM = SSD_WIDTH + 2 * SSD_GROUPS * SSD_STATE
N_EXPERTS = 32
TOP_K = 4
D_EXPERT = D_MODEL
SWIGLU_LIMIT = 7.0
SWIGLU_ALPHA = 1.702
PLE_DIM = 256
EPS = 1e-6

LANES = 128
SUBLANES = 8
VMEM_LIMIT = 56 * 1024 * 1024
NEG_BIG = -1e30

_DT_PAD = LANES
_IN_CUTS = (SB_WIDTH, SB_WIDTH, SB_WIDTH, SSD_WIDTH, CONV_DIM, _DT_PAD)
_IN_COLS = sum(_IN_CUTS)

_NT = (((1,), (1,)), ((), ()))
_TN = (((0,), (0,)), ((), ()))


def _row_tile(n, tm):
    tm = min(tm, n)
    while n % tm:
        tm -= SUBLANES
    return tm


def _cparams(sem):
    return pltpu.CompilerParams(dimension_semantics=sem, vmem_limit_bytes=VMEM_LIMIT)


def _rms(x, g):
    return x * lax.rsqrt(jnp.mean(x * x, axis=-1, keepdims=True) + EPS) * g


def _split3(x):
    hi = x.astype(BF16)
    r1 = x - hi.astype(F32)
    mid = r1.astype(BF16)
    lo = (r1 - mid.astype(F32)).astype(BF16)
    return hi, mid, lo


def _dot3_l(x, m):
    hi, mid, lo = _split3(x)
    d = lambda a: jnp.dot(a, m, preferred_element_type=F32)
    return d(hi) + d(mid) + d(lo)


def _dot3_r(m, x):
    hi, mid, lo = _split3(x)
    d = lambda a: jnp.dot(m, a, preferred_element_type=F32)
    return d(hi) + d(mid) + d(lo)


def _inproj_kernel(x_ref, g_ref, w_ref, q_ref, k_ref, v_ref, z_ref, xbc_ref, dt_ref):
    a = _rms(x_ref[...], g_ref[...]).astype(BF16)
    c0 = 0
    for ref, width in zip((q_ref, k_ref, v_ref, z_ref, xbc_ref, dt_ref), _IN_CUTS):
        ref[...] = jnp.dot(a, w_ref[:, c0:c0 + width], preferred_element_type=F32)
        c0 += width


def _inproj(x, g, w_pad, tm=256):
    n = x.shape[0]
    tm = _row_tile(n, tm)
    row = lambda i: (i, 0)
    fixed = lambda i: (0, 0)
    return pl.pallas_call(
        _inproj_kernel,
        grid=(n // tm,),
        in_specs=[pl.BlockSpec((tm, D_MODEL), row),
                  pl.BlockSpec((1, D_MODEL), fixed),
                  pl.BlockSpec((D_MODEL, _IN_COLS), fixed)],
        out_specs=[pl.BlockSpec((tm, c), row) for c in _IN_CUTS],
        out_shape=[jax.ShapeDtypeStruct((n, c), F32) for c in _IN_CUTS],
        compiler_params=_cparams(("parallel",)),
        name="inproj",
    )(x, g, w_pad)


def _softplus(z):
    return jnp.maximum(z, 0.0) + jnp.log(1.0 + jnp.exp(-jnp.abs(z)))


def _attn_prompt_kernel(bias_ref, q_ref, k_ref, v_ref, tri_ref, o_ref, acc_ref, car_ref, *, blk):
    hp = pl.program_id(1)
    qi = pl.program_id(2)
    lane = lax.broadcasted_iota(jnp.int32, (blk, LANES), 1)
    first = lane < SB_HEAD_DIM
    q2 = q_ref[0] * SB_SCALE
    qs = jnp.concatenate([jnp.where(first, q2, 0.0), jnp.where(first, 0.0, q2)], axis=0).astype(BF16)
    rows = lax.broadcasted_iota(jnp.int32, (2 * blk, 1), 0)
    bias = jnp.where(rows < blk, bias_ref[hp * 2], bias_ref[hp * 2 + 1])
    tri = tri_ref[...]

    def block(kj, acc, car, causal):
        start = pl.multiple_of(kj * blk, blk)
        kb = k_ref[0, pl.ds(start, blk), :].astype(BF16)
        vb = v_ref[0, pl.ds(start, blk), :].astype(BF16)
        s = lax.dot_general(qs, kb, _NT, preferred_element_type=F32)
        sp = _softplus(s + bias)
        if causal is not None:
            sp = jnp.where(causal, sp, 0.0)
        later = jnp.dot(sp.astype(BF16), tri, preferred_element_type=F32)
        w = jnp.exp((s + (bias - car)) - sp - later)
        if causal is not None:
            w = jnp.where(causal, w, 0.0)
        acc = acc + jnp.dot(w.astype(BF16), vb, preferred_element_type=F32)
        car = car + jnp.sum(sp, axis=-1, keepdims=True)
        return acc, car

    r = lax.broadcasted_iota(jnp.int32, (2 * blk, blk), 0)
    c = lax.broadcasted_iota(jnp.int32, (2 * blk, blk), 1)
    causal = c < jnp.where(r < blk, r, r - blk)
    acc, car = block(qi, jnp.zeros((2 * blk, LANES), F32), jnp.zeros((2 * blk, 1), F32), causal)
    acc_ref[...] = acc
    car_ref[...] = car

    def run(first_block, count):
        acc, car = acc_ref[...], car_ref[...]
        for u in range(count):
            acc, car = block(first_block - u, acc, car, None)
        acc_ref[...] = acc
        car_ref[...] = car

    unroll = 4
    n_main = qi // unroll

    def main(t, carry):
        run(qi - 1 - unroll * t, unroll)
        return carry

    def tail(t, carry):
        run(qi - 1 - unroll * n_main - t, 1)
        return carry

    lax.fori_loop(0, n_main, main, 0)
    lax.fori_loop(0, qi - unroll * n_main, tail, 0)

    o_ref[0] = jnp.where(first, acc_ref[0:blk, :], acc_ref[blk:, :])


def _strict_upper_sum_matrix(n):
    j = lax.broadcasted_iota(jnp.int32, (n, n), 0)
    s = lax.broadcasted_iota(jnp.int32, (n, n), 1)
    return (j > s).astype(BF16)


def _attn_prompt(q, k, v, bias, blk=256):
    b, t, _ = q.shape
    blk = min(blk, t)
    tri = _strict_upper_sum_matrix(blk)
    grid_spec = pltpu.PrefetchScalarGridSpec(
        num_scalar_prefetch=0,
        grid=(b, SB_WIDTH // LANES, t // blk),
        in_specs=[pl.BlockSpec(memory_space=pltpu.SMEM),
                  pl.BlockSpec((1, blk, LANES), lambda bi, h, i: (bi, i, h)),
                  pl.BlockSpec((1, t, LANES), lambda bi, h, i: (bi, 0, h)),
                  pl.BlockSpec((1, t, LANES), lambda bi, h, i: (bi, 0, h)),
                  pl.BlockSpec((blk, blk), lambda bi, h, i: (0, 0))],
        out_specs=pl.BlockSpec((1, blk, LANES), lambda bi, h, i: (bi, i, h)),
        scratch_shapes=[pltpu.VMEM((2 * blk, LANES), F32), pltpu.VMEM((2 * blk, 1), F32)],
    )
    return pl.pallas_call(
        functools.partial(_attn_prompt_kernel, blk=blk),
        grid_spec=grid_spec,
        out_shape=jax.ShapeDtypeStruct((b, t, SB_WIDTH), F32),
        compiler_params=_cparams(("parallel", "parallel", "arbitrary")),
        name="attn_prompt",
    )(bias, q, k, v, tri)


def _attn_sample_kernel(pt_ref, q_ref, kn_ref, vn_ref, bias_ref, trio_ref, *rest, pps, page, tq):
    del pt_ref
    kp_refs = rest[:pps]
    vp_refs = rest[pps:2 * pps]
    o_ref = rest[2 * pps]
    qbd_ref, acc_ref, car_ref = rest[2 * pps + 1:]
    j = pl.program_id(1)
    nj = pl.num_programs(1)
    nq = SB_HEADS * tq

    def attend(kts, vts, mask):
        n = len(kts)
        kcat = jnp.concatenate(kts, axis=1).astype(BF16)
        bias = jnp.concatenate([bias_ref[...]] * n, axis=1)
        z = jnp.dot(qbd_ref[...], kcat, preferred_element_type=F32) + bias
        sp = _softplus(z)
        if mask is not None:
            sp = jnp.where(mask, sp, 0.0)
        sp_rows = jnp.concatenate([sp[:, i * page:(i + 1) * page] for i in range(n)], axis=0)
        hi = sp_rows.astype(BF16)
        mid = (sp_rows - hi.astype(F32)).astype(BF16)
        sums = jnp.dot(jnp.concatenate([hi, mid], axis=1), trio_ref[...], preferred_element_type=F32)
        car = car_ref[...]
        later = []
        for i in range(n):
            part = sums[i * nq:(i + 1) * nq, :]
            later.append(part[:, :page] + car)
            car = car + part[:, page:]
        car_ref[...] = car
        w = jnp.exp(z - sp - jnp.concatenate(later, axis=1))
        if mask is not None:
            w = jnp.where(mask, w, 0.0)
        vcat = jnp.concatenate(vts, axis=1).astype(BF16)
        acc_ref[...] += lax.dot_general(w.astype(BF16), vcat, _NT, preferred_element_type=F32)

    @pl.when(j == 0)
    def _():
        q = q_ref[0] * SB_SCALE
        qt = jnp.concatenate([q] * SB_HEADS, axis=0)
        r = lax.broadcasted_iota(jnp.int32, (nq, SB_WIDTH), 0)
        c = lax.broadcasted_iota(jnp.int32, (nq, SB_WIDTH), 1)
        qbd_ref[...] = jnp.where(r // tq == c // SB_HEAD_DIM, qt, 0.0).astype(BF16)
        acc_ref[...] = jnp.zeros_like(acc_ref)
        car_ref[...] = jnp.zeros_like(car_ref)
        fill = jnp.zeros((page - tq, SB_WIDTH), F32)
        kn_t = jnp.concatenate([kn_ref[0], fill], axis=0).T
        vn_t = jnp.concatenate([vn_ref[0], fill], axis=0).T
        query = lax.broadcasted_iota(jnp.int32, (nq, page), 0) % tq
        key = lax.broadcasted_iota(jnp.int32, (nq, page), 1)
        attend([kn_t], [vn_t], key < query)

    flat = lambda ref: ref[0].reshape(SB_WIDTH, page)
    attend([flat(r) for r in kp_refs], [flat(r) for r in vp_refs], None)

    @pl.when(j == nj - 1)
    def _():
        lane = lax.broadcasted_iota(jnp.int32, (tq, SB_WIDTH), 1)
        out = jnp.zeros((tq, SB_WIDTH), F32)
        for h in range(SB_HEADS):
            rows = acc_ref[h * tq:(h + 1) * tq, :]
            out = jnp.where(lane // SB_HEAD_DIM == h, rows, out)
        o_ref[0] = out


def _attn_sample(q, k_new, v_new, cache_k, cache_v, page_table, bias, pps=8):
    bs, tq, _ = q.shape
    n_pool, page = cache_k.shape[0], cache_k.shape[1]
    n_pages = page_table.shape[1]
    pps = min(pps, n_pages)
    nq = SB_HEADS * tq
    assert n_pages % pps == 0 and tq % SUBLANES == 0
    ck = jnp.transpose(cache_k, (0, 2, 3, 1))
    cv = jnp.transpose(cache_v, (0, 2, 3, 1))
    bias_rows = jnp.broadcast_to(jnp.repeat(bias, tq)[:, None], (nq, page))
    jj = lax.broadcasted_iota(jnp.int32, (page, page), 0)
    ss = lax.broadcasted_iota(jnp.int32, (page, page), 1)
    half = jnp.concatenate([(jj > ss).astype(BF16), jnp.ones((page, page), BF16)], axis=1)
    trio = jnp.concatenate([half, half], axis=0)

    def page_spec(i):
        return pl.BlockSpec((1, SB_HEADS, SB_HEAD_DIM, page),
                            lambda b, j, pt: (pt[b, n_pages - 1 - (j * pps + i)], 0, 0, 0))

    per_seq = pl.BlockSpec((1, tq, SB_WIDTH), lambda b, j, pt: (b, 0, 0))
    grid_spec = pltpu.PrefetchScalarGridSpec(
        num_scalar_prefetch=1,
        grid=(bs, n_pages // pps),
        in_specs=[per_seq, per_seq, per_seq,
                  pl.BlockSpec((nq, page), lambda b, j, pt: (0, 0)),
                  pl.BlockSpec((2 * page, 2 * page), lambda b, j, pt: (0, 0))]
                 + [page_spec(i) for i in range(pps)] * 2,
        out_specs=per_seq,
        scratch_shapes=[pltpu.VMEM((nq, SB_WIDTH), BF16),
                        pltpu.VMEM((nq, SB_WIDTH), F32),
                        pltpu.VMEM((nq, page), F32)],
    )
    return pl.pallas_call(
        functools.partial(_attn_sample_kernel, pps=pps, page=page, tq=tq),
        grid_spec=grid_spec,
        out_shape=jax.ShapeDtypeStruct((bs, tq, SB_WIDTH), F32),
        compiler_params=_cparams(("parallel", "arbitrary")),
        name="attn_sample",
    )(page_table, q, k_new, v_new, bias_rows, trio, *([ck] * pps), *([cv] * pps))


def _ssd_kernel(xbc_ref, dt_ref, h0_ref, c0_ref, cw_ref, cb_ref, dtb_ref, a_ref, dsk_ref,
                tril_ref, triu_ref, y_ref, hout_ref, xext_ref, h_ref, *, chunk, tin):
    ci = pl.program_id(1)
    nc = pl.num_programs(1)
    pad = SUBLANES

    @pl.when(ci == 0)
    def _():
        h_ref[...] = h0_ref[0]
        xext_ref[0:pad, :] = c0_ref[0]

    @pl.when(ci > 0)
    def _():
        xext_ref[0:pad, :] = xext_ref[chunk:chunk + pad, :]

    if tin < chunk:
        xext_ref[pad:, :] = jnp.zeros((chunk, CONV_DIM), F32)
    xext_ref[pad:pad + tin, :] = xbc_ref[0]

    conv = cb_ref[...] + sum(
        xext_ref[pad - (SSD_CONV - 1) + jj: pad - (SSD_CONV - 1) + jj + chunk, :] * cw_ref[jj:jj + 1, :]
        for jj in range(SSD_CONV))
    u = conv * jax.nn.sigmoid(conv)

    if tin < chunk:
        dt_in = jnp.concatenate([dt_ref[0], jnp.zeros((chunk - tin, LANES), F32)], axis=0)
    else:
        dt_in = dt_ref[0]
    dt = jax.nn.softplus(dt_in + dtb_ref[...])
    if tin < chunk:
        live = lax.broadcasted_iota(jnp.int32, (chunk, LANES), 0) < tin
        dt = jnp.where(live, dt, 0.0)
    a = dt * a_ref[...]
    acs = _dot3_r(tril_ref[...], a)
    acs_t = _dot3_l(a.T, triu_ref[...])
    acs_last = acs[chunk - 1:chunk, :]
    decay_in = jnp.exp(acs_last - acs)
    e_acs = jnp.exp(acs)
    cdec_t = jnp.exp(acs_t[:, chunk - 1:chunk])

    lrow = lax.broadcasted_iota(jnp.int32, (chunk, chunk), 0)
    scol = lax.broadcasted_iota(jnp.int32, (chunk, chunk), 1)
    causal = scol <= lrow
    lane = lax.broadcasted_iota(jnp.int32, (chunk, LANES), 1)
    first = lane < SSD_HEAD_DIM
    srow = lax.broadcasted_iota(jnp.int32, (LANES, LANES), 0)

    hpg = SSD_HEADS // SSD_GROUPS
    for g in range(SSD_GROUPS):
        b0 = SSD_WIDTH + g * SSD_STATE
        c0 = SSD_WIDTH + SSD_GROUPS * SSD_STATE + g * SSD_STATE
        bm = u[:, b0:b0 + SSD_STATE].astype(BF16)
        cm = u[:, c0:c0 + SSD_STATE].astype(BF16)
        cb = lax.dot_general(cm, bm, _NT, preferred_element_type=F32)
        for pr in range(hpg // 2):
            h0i = g * hpg + 2 * pr
            h1i = h0i + 1
            x2 = u[:, h0i * SSD_HEAD_DIM:(h0i + 2) * SSD_HEAD_DIM]
            pick = lambda m: jnp.where(first, m[:, h0i:h0i + 1], m[:, h1i:h1i + 1])
            xdt = x2 * pick(dt)
            xdt_b = xdt.astype(BF16)
            ys = []
            for hi in (h0i, h1i):
                seg = acs[:, hi:hi + 1] - acs_t[hi:hi + 1, :]
                lmat = jnp.exp(jnp.where(causal, seg, -jnp.inf))
                ys.append(jnp.dot((cb * lmat).astype(BF16), xdt_b, preferred_element_type=F32))
            y_diag = jnp.where(first, ys[0], ys[1])
            hrows = slice(h0i * SSD_HEAD_DIM, (h0i + 2) * SSD_HEAD_DIM)
            h_prev = h_ref[hrows, :]
            y_off = lax.dot_general(cm, h_prev.astype(BF16), _NT, preferred_element_type=F32) * pick(e_acs)
            states = lax.dot_general((xdt * pick(decay_in)).astype(BF16), bm, _TN,
                                     preferred_element_type=F32)
            cdec = jnp.where(srow < SSD_HEAD_DIM, cdec_t[h0i:h0i + 1, :], cdec_t[h1i:h1i + 1, :])
            h_ref[hrows, :] = cdec * h_prev + states
            y2 = y_diag + y_off + dsk_ref[:, h0i * SSD_HEAD_DIM:(h0i + 2) * SSD_HEAD_DIM] * x2
            y_ref[0, :, h0i * SSD_HEAD_DIM:(h0i + 2) * SSD_HEAD_DIM] = y2[0:tin, :]

    @pl.when(ci == nc - 1)
    def _():
        hout_ref[0] = h_ref[...]


def _ssd(xbc, dt_raw, h0, conv0, conv_w, conv_b, dt_bias, a_log, d_skip, chunk=128):
    b, t, _ = xbc.shape
    tin = min(chunk, t)
    nc = t // tin
    pad_lanes = lambda v: jnp.concatenate([v.astype(F32), jnp.zeros((LANES - v.shape[0],), F32)])[None, :]
    dtb = pad_lanes(dt_bias)
    a_neg = pad_lanes(-jnp.exp(a_log.astype(F32)))
    dsk = jnp.repeat(d_skip.astype(F32), SSD_HEAD_DIM)[None, :]
    r = lax.broadcasted_iota(jnp.int32, (chunk, chunk), 0)
    c = lax.broadcasted_iota(jnp.int32, (chunk, chunk), 1)
    tril = (c <= r).astype(BF16)
    triu = (r <= c).astype(BF16)
    fixed = lambda bi, ci: (0, 0)
    per_b = lambda bi, ci: (bi, 0, 0)
    y, h_fin = pl.pallas_call(
        functools.partial(_ssd_kernel, chunk=chunk, tin=tin),
        grid=(b, nc),
        in_specs=[pl.BlockSpec((1, tin, CONV_DIM), lambda bi, ci: (bi, ci, 0)),
                  pl.BlockSpec((1, tin, LANES), lambda bi, ci: (bi, ci, 0)),
                  pl.BlockSpec((1, SSD_WIDTH, SSD_STATE), per_b),
                  pl.BlockSpec((1, SUBLANES, CONV_DIM), per_b),
                  pl.BlockSpec((SSD_CONV, CONV_DIM), fixed),
                  pl.BlockSpec((1, CONV_DIM), fixed),
                  pl.BlockSpec((1, LANES), fixed),
                  pl.BlockSpec((1, LANES), fixed),
                  pl.BlockSpec((1, SSD_WIDTH), fixed),
                  pl.BlockSpec((chunk, chunk), fixed),
                  pl.BlockSpec((chunk, chunk), fixed)],
        out_specs=[pl.BlockSpec((1, tin, SSD_WIDTH), lambda bi, ci: (bi, ci, 0)),
                   pl.BlockSpec((1, SSD_WIDTH, SSD_STATE), per_b)],
        out_shape=[jax.ShapeDtypeStruct((b, t, SSD_WIDTH), F32),
                   jax.ShapeDtypeStruct((b, SSD_WIDTH, SSD_STATE), F32)],
        scratch_shapes=[pltpu.VMEM((SUBLANES + chunk, CONV_DIM), F32),
                        pltpu.VMEM((SSD_WIDTH, SSD_STATE), F32)],
        compiler_params=_cparams(("parallel", "arbitrary")),
        name="ssd",
    )(xbc, dt_raw, h0, conv0, conv_w, conv_b[None, :], dtb, a_neg, dsk, tril, triu)
    return y, h_fin


def _mixout_kernel(o_ref, y_ref, z_ref, h_ref, ga_ref, gs_ref, wo_ref, gf_ref, wr_ref, br_ref,
                   h1_ref, f_ref, logit_ref):
    attn = _rms(o_ref[...], ga_ref[...]).astype(BF16)
    z = z_ref[...]
    yg = y_ref[...] * (z * jax.nn.sigmoid(z))
    gw = SSD_WIDTH // SSD_GROUPS
    parts = []
    for g in range(SSD_GROUPS):
        part = yg[:, g * gw:(g + 1) * gw]
        parts.append(part * lax.rsqrt(jnp.mean(part * part, axis=-1, keepdims=True) + EPS))
    yn = (jnp.concatenate(parts, axis=-1) * gs_ref[...]).astype(BF16)
    mix = (jnp.dot(attn, wo_ref[0:SB_WIDTH, :], preferred_element_type=F32)
           + jnp.dot(yn, wo_ref[SB_WIDTH:, :], preferred_element_type=F32))
    h1 = h_ref[...] + mix
    h1_ref[...] = h1
    f = _rms(h1, gf_ref[...])
    f_ref[...] = f.astype(BF16)

    f_hi, f_mid, f_lo = _split3(f)
    w_hi, w_mid, w_lo = wr_ref[0], wr_ref[1], wr_ref[2]
    d = lambda a, b: jnp.dot(a, b, preferred_element_type=F32)
    logit_ref[...] = (d(f_hi, w_hi) + (d(f_hi, w_mid) + d(f_mid, w_hi))
                      + (d(f_hi, w_lo) + d(f_mid, w_mid) + d(f_lo, w_hi))) + br_ref[...]


def _mixout(o, y, z, h, g_attn, g_ssd, w_out_b, g_ffn, w_router3, b_router_pad, tm=256):
    n = h.shape[0]
    tm = _row_tile(n, tm)
    row = lambda i: (i, 0)
    fixed = lambda i: (0, 0)
    return pl.pallas_call(
        _mixout_kernel,
        grid=(n // tm,),
        in_specs=[pl.BlockSpec((tm, SB_WIDTH), row),
                  pl.BlockSpec((tm, SSD_WIDTH), row),
                  pl.BlockSpec((tm, SSD_WIDTH), row),
                  pl.BlockSpec((tm, D_MODEL), row),
                  pl.BlockSpec((1, SB_WIDTH), fixed),
                  pl.BlockSpec((1, SSD_WIDTH), fixed),
                  pl.BlockSpec((SB_WIDTH + SSD_WIDTH, D_MODEL), fixed),
                  pl.BlockSpec((1, D_MODEL), fixed),
                  pl.BlockSpec((3, D_MODEL, LANES), lambda i: (0, 0, 0)),
                  pl.BlockSpec((1, LANES), fixed)],
        out_specs=[pl.BlockSpec((tm, D_MODEL), row),
                   pl.BlockSpec((tm, D_MODEL), row),
                   pl.BlockSpec((tm, LANES), row)],
        out_shape=[jax.ShapeDtypeStruct((n, D_MODEL), F32),
                   jax.ShapeDtypeStruct((n, D_MODEL), BF16),
                   jax.ShapeDtypeStruct((n, LANES), F32)],
        compiler_params=_cparams(("parallel",)),
        name="mixout",
    )(o, y, z, h, g_attn[None, :], g_ssd[None, :], w_out_b, g_ffn[None, :], w_router3, b_router_pad)


def _swiglu_packed(gu):
    half = gu.shape[1] // 2
    lane = lax.broadcasted_iota(jnp.int32, (gu.shape[0], half), 1)
    even = (lane % 2) == 0
    lo, hi = gu[:, :half], gu[:, half:]
    gate = jnp.where(even, lo, pltpu.roll(hi, 1, 1))
    up = jnp.where(even, pltpu.roll(lo, half - 1, 1), hi)
    gate = jnp.minimum(gate, SWIGLU_LIMIT)
    up = jnp.clip(up, -SWIGLU_LIMIT, SWIGLU_LIMIT)
    return (up + 1.0) * (gate * jax.nn.sigmoid(SWIGLU_ALPHA * gate))


MOE_TILE = 512
MOE_PIECE = 16
MOE_ROWS = 256


def _moe_pieces(tile):
    return tile * TOP_K // MOE_PIECE + N_EXPERTS


def _route_kernel(lg_ref, below_ref, before_ref, pos_ref, gate_ref, post_ref, cnt_ref):
    logits = lg_ref[...]
    tile = logits.shape[0]
    lane = lax.broadcasted_iota(jnp.int32, logits.shape, 1)
    work = logits
    tops, sels = [], []
    for _ in range(TOP_K):
        m = jnp.max(work, axis=-1, keepdims=True)
        idx = jnp.min(jnp.where(work == m, lane, LANES), axis=-1, keepdims=True)
        sel = lane == idx
        tops.append(m)
        sels.append(sel)
        work = jnp.where(sel, -jnp.inf, work)
    exps = [jnp.exp(t - tops[0]) for t in tops]
    denom = exps[0] + exps[1] + exps[2] + exps[3]

    member = jnp.zeros(logits.shape, F32)
    for sel in sels:
        member = jnp.where(sel, 1.0, member)
    rank = jnp.dot(below_ref[...], member.astype(BF16), preferred_element_type=F32)
    cnt = jnp.sum(member, axis=0, keepdims=True)
    pieces = jnp.floor((cnt + (MOE_PIECE - 1)) * (1.0 / MOE_PIECE))
    pieces8 = jnp.broadcast_to(pieces, (SUBLANES, LANES))
    start = jnp.dot(pieces8.astype(BF16), before_ref[...], preferred_element_type=F32)[0:1, :] * MOE_PIECE
    slot = start + rank
    pos = jnp.full(logits.shape, -1.0, F32)
    gate = jnp.zeros(logits.shape, F32)
    for k in range(TOP_K):
        pk = jnp.sum(jnp.where(sels[k], slot, 0.0), axis=-1, keepdims=True)
        pos = jnp.where(lane == k, pk, pos)
        gate = jnp.where(lane == k, exps[k] / denom, gate)
    pos_ref[...] = pos
    gate_ref[...] = gate
    post_ref[0] = pos.T[0:SUBLANES, :]
    cnt_ref[0] = pieces8


def _route(logits, tile):
    n = logits.shape[0]
    nt = n // tile
    r = lax.broadcasted_iota(jnp.int32, (tile, tile), 0)
    c = lax.broadcasted_iota(jnp.int32, (tile, tile), 1)
    below = (c < r).astype(BF16)
    e0 = lax.broadcasted_iota(jnp.int32, (LANES, LANES), 0)
    e1 = lax.broadcasted_iota(jnp.int32, (LANES, LANES), 1)
    before = (e0 < e1).astype(BF16)
    row = lambda i: (i, 0)
    fixed = lambda i: (0, 0)
    return pl.pallas_call(
        _route_kernel,
        grid=(nt,),
        in_specs=[pl.BlockSpec((tile, LANES), row),
                  pl.BlockSpec((tile, tile), fixed),
                  pl.BlockSpec((LANES, LANES), fixed)],
        out_specs=[pl.BlockSpec((tile, LANES), row),
                   pl.BlockSpec((tile, LANES), row),
                   pl.BlockSpec((1, SUBLANES, tile), lambda i: (i, 0, 0)),
                   pl.BlockSpec((1, SUBLANES, LANES), lambda i: (i, 0, 0))],
        out_shape=[jax.ShapeDtypeStruct((n, LANES), F32),
                   jax.ShapeDtypeStruct((n, LANES), F32),
                   jax.ShapeDtypeStruct((nt, SUBLANES, tile), F32),
                   jax.ShapeDtypeStruct((nt, SUBLANES, LANES), F32)],
        compiler_params=_cparams(("parallel",)),
        name="moe_route",
    )(logits, below, before)


def _piece_copy(src, src_row, dst, dst_row, sem):
    return pltpu.make_async_copy(src.at[pl.ds(pl.multiple_of(src_row, MOE_PIECE), MOE_PIECE)],
                                 dst.at[pl.ds(pl.multiple_of(dst_row, MOE_PIECE), MOE_PIECE)], sem)


def _moe_gather_kernel(dest_ref, np_ref, f_ref, post_ref, xs_in_ref, xs_ref, xl_ref, sem, *, pmax):
    del xs_in_ref
    i = pl.program_id(0)
    tile = f_ref.shape[0]
    rows = lax.broadcasted_iota(jnp.int32, (pmax * MOE_PIECE, tile), 0).astype(F32)
    post = post_ref[0]
    onehot = jnp.zeros(rows.shape, F32)
    for k in range(TOP_K):
        onehot = onehot + jnp.where(rows == post[k:k + 1, :], 1.0, 0.0)
    sorted_rows = jnp.dot(onehot.astype(BF16), f_ref[...], preferred_element_type=F32).astype(BF16)

    def copy(tile_idx, p):
        return _piece_copy(xl_ref, p * MOE_PIECE, xs_ref, dest_ref[tile_idx * pmax + p], sem)

    def wait_tile(tile_idx):
        def wait(p, c):
            copy(tile_idx, p).wait()
            return c
        lax.fori_loop(0, np_ref[tile_idx], wait, 0)

    @pl.when(i > 0)
    def _():
        wait_tile(i - 1)

    xl_ref[...] = sorted_rows

    def start(p, c):
        copy(i, p).start()
        return c

    lax.fori_loop(0, np_ref[i], start, 0)

    @pl.when(i == pl.num_programs(0) - 1)
    def _():
        wait_tile(i)


def _moe_gather(f, post, dest, npieces, total_rows, tile):
    n = f.shape[0]
    pmax = _moe_pieces(tile)
    xs0 = jnp.zeros((total_rows, D_MODEL), BF16)
    grid_spec = pltpu.PrefetchScalarGridSpec(
        num_scalar_prefetch=2,
        grid=(n // tile,),
        in_specs=[pl.BlockSpec((tile, D_MODEL), lambda i, d, c: (i, 0)),
                  pl.BlockSpec((1, SUBLANES, tile), lambda i, d, c: (i, 0, 0)),
                  pl.BlockSpec(memory_space=pl.ANY)],
        out_specs=pl.BlockSpec(memory_space=pl.ANY),
        scratch_shapes=[pltpu.VMEM((pmax * MOE_PIECE, D_MODEL), BF16), pltpu.SemaphoreType.DMA],
    )
    return pl.pallas_call(
        functools.partial(_moe_gather_kernel, pmax=pmax),
        grid_spec=grid_spec,
        out_shape=jax.ShapeDtypeStruct((total_rows, D_MODEL), BF16),
        input_output_aliases={4: 0},
        compiler_params=_cparams(("arbitrary",)),
        name="moe_gather",
    )(dest, npieces, f, post, xs0)


def _moe_expert_kernel(te_ref, tv_ref, xs_ref, wgu_ref, bgu_ref, wd_ref, bd_ref, ys_ref,
                       wgu_b, wd_f, wd_b):
    t = pl.program_id(0)
    e = te_ref[t]
    prev = te_ref[jnp.maximum(t - 1, 0)]

    @pl.when((t == 0) | (e != prev))
    def _():
        wgu_b[...] = wgu_ref[0].astype(BF16)
        half = D_EXPERT // 2
        for c in range(D_MODEL // LANES):
            cols = slice(c * LANES, (c + 1) * LANES)
            wd_f[c, pl.ds(0, half, stride=2), :] = wd_ref[0, 0:half, cols]
            wd_f[c, pl.ds(1, half, stride=2), :] = wd_ref[0, half:, cols]
            wd_b[:, cols] = wd_f[c].astype(BF16)

    @pl.when(tv_ref[t] == 1)
    def _():
        gu = jnp.dot(xs_ref[...], wgu_b[...], preferred_element_type=F32) + bgu_ref[0]
        act = _swiglu_packed(gu).astype(BF16)
        y = jnp.dot(act, wd_b[...], preferred_element_type=F32) + bd_ref[0]
        ys_ref[...] = y.astype(BF16)

    @pl.when(tv_ref[t] == 0)
    def _():
        ys_ref[...] = jnp.zeros_like(ys_ref)


def _moe_experts(xs, tile_expert, tile_valid, w_gate_up, b_gate_up, w_down, b_down):
    rows = xs.shape[0]
    per_e = lambda t, te, tv: (te[t], 0, 0)
    row = lambda t, te, tv: (t, 0)
    grid_spec = pltpu.PrefetchScalarGridSpec(
        num_scalar_prefetch=2,
        grid=(rows // MOE_ROWS,),
        in_specs=[pl.BlockSpec((MOE_ROWS, D_MODEL), row),
                  pl.BlockSpec((1, D_MODEL, 2 * D_EXPERT), per_e),
                  pl.BlockSpec((1, 1, 2 * D_EXPERT), per_e),
                  pl.BlockSpec((1, D_EXPERT, D_MODEL), per_e),
                  pl.BlockSpec((1, 1, D_MODEL), per_e)],
        out_specs=pl.BlockSpec((MOE_ROWS, D_MODEL), row),
        scratch_shapes=[pltpu.VMEM((D_MODEL, 2 * D_EXPERT), BF16),
                        pltpu.VMEM((D_MODEL // LANES, D_EXPERT, LANES), F32),
                        pltpu.VMEM((D_EXPERT, D_MODEL), BF16)],
    )
    return pl.pallas_call(
        _moe_expert_kernel,
        grid_spec=grid_spec,
        out_shape=jax.ShapeDtypeStruct((rows, D_MODEL), BF16),
        compiler_params=_cparams(("arbitrary",)),
        name="moe_experts",
    )(tile_expert, tile_valid, xs, w_gate_up, b_gate_up[:, None, :], w_down, b_down[:, None, :])


def _moe_combine_kernel(dest_ref, np_ref, pos_ref, gate_ref, ys_ref, o_ref, yl_ref, sem, *, pmax):
    i = pl.program_id(0)
    slot = i % 2

    def copy(tile_idx, p):
        s = tile_idx % 2
        return _piece_copy(ys_ref, dest_ref[tile_idx * pmax + p], yl_ref.at[s], p * MOE_PIECE, sem.at[s])

    def start_tile(tile_idx):
        def start(p, c):
            copy(tile_idx, p).start()
            return c
        lax.fori_loop(0, np_ref[tile_idx], start, 0)

    @pl.when(i == 0)
    def _():
        yl_ref[...] = jnp.zeros_like(yl_ref)
        start_tile(0)

    @pl.when(i + 1 < pl.num_programs(0))
    def _():
        start_tile(i + 1)

    def wait(p, c):
        copy(i, p).wait()
        return c

    lax.fori_loop(0, np_ref[i], wait, 0)

    tile = pos_ref.shape[0]
    cols = lax.broadcasted_iota(jnp.int32, (tile, pmax * MOE_PIECE), 1).astype(F32)
    pos = pos_ref[...]
    gate = gate_ref[...]
    weights = jnp.zeros(cols.shape, F32)
    for k in range(TOP_K):
        weights = weights + jnp.where(cols == pos[:, k:k + 1], gate[:, k:k + 1], 0.0)
    o_ref[...] = jnp.dot(weights.astype(BF16), yl_ref[slot], preferred_element_type=F32)


def _moe_combine(ys, pos, gate, dest, npieces, tile):
    n = pos.shape[0]
    pmax = _moe_pieces(tile)
    grid_spec = pltpu.PrefetchScalarGridSpec(
        num_scalar_prefetch=2,
        grid=(n // tile,),
        in_specs=[pl.BlockSpec((tile, LANES), lambda i, d, c: (i, 0)),
                  pl.BlockSpec((tile, LANES), lambda i, d, c: (i, 0)),
                  pl.BlockSpec(memory_space=pl.ANY)],
        out_specs=pl.BlockSpec((tile, D_MODEL), lambda i, d, c: (i, 0)),
        scratch_shapes=[pltpu.VMEM((2, pmax * MOE_PIECE, D_MODEL), BF16), pltpu.SemaphoreType.DMA((2,))],
    )
    return pl.pallas_call(
        functools.partial(_moe_combine_kernel, pmax=pmax),
        grid_spec=grid_spec,
        out_shape=jax.ShapeDtypeStruct((n, D_MODEL), F32),
        compiler_params=_cparams(("arbitrary",)),
        name="moe_combine",
    )(dest, npieces, pos, gate, ys)


def _moe_tile(n):
    tile = min(MOE_TILE, n)
    while n % tile or tile % MOE_PIECE:
        tile -= SUBLANES
    return tile


def _moe(f, logits, w_gate_up, b_gate_up, w_down, b_down):
    n = f.shape[0]
    tile = _moe_tile(n)
    nt = n // tile
    pmax = _moe_pieces(tile)
    pos, gate, post, cnt = _route(logits, tile)

    i32 = jnp.int32
    rows = cnt[:, 0, :N_EXPERTS].astype(i32) * MOE_PIECE
    lstart = jnp.cumsum(rows, axis=1) - rows
    group = jnp.sum(rows, axis=0)
    group_pad = (group + MOE_ROWS - 1) // MOE_ROWS * MOE_ROWS
    goff = jnp.cumsum(group_pad) - group_pad
    gbase = goff[None, :] + jnp.cumsum(rows, axis=0) - rows
    npieces = (jnp.sum(rows, axis=1) // MOE_PIECE).astype(i32)
    p_row = jnp.arange(pmax, dtype=i32) * MOE_PIECE
    p3 = p_row[None, :, None]
    owns = (p3 >= lstart[:, None, :]) & (p3 < (lstart + rows)[:, None, :])
    dest = jnp.sum(jnp.where(owns, (gbase - lstart)[:, None, :] + p3, 0), axis=-1).astype(i32).reshape(-1)

    n_tiles = (n * TOP_K + nt * N_EXPERTS * (MOE_PIECE - 1)) // MOE_ROWS + N_EXPERTS + 1
    total_rows = n_tiles * MOE_ROWS
    t_row = jnp.arange(n_tiles, dtype=i32) * MOE_ROWS
    gend = goff + group_pad
    tile_expert = jnp.minimum(jnp.sum(t_row[:, None] >= gend[None, :], axis=-1), N_EXPERTS - 1).astype(i32)
    tile_valid = (t_row < gend[-1]).astype(i32)

    xs = _moe_gather(f, post, dest, npieces, total_rows, tile)
    ys = _moe_experts(xs, tile_expert, tile_valid, w_gate_up, b_gate_up, w_down, b_down)
    return _moe_combine(ys, pos, gate, dest, npieces, tile)


def _ple_kernel(h1_ref, moe_ref, p_ref, gin_ref, wg_ref, wp_ref, gout_ref, gfin_ref, y_ref, *, final):
    h2 = h1_ref[...] + moe_ref[...]
    c = _rms(h2, gin_ref[...]).astype(BF16)
    gate = jax.nn.sigmoid(jnp.dot(c, wg_ref[...], preferred_element_type=F32))
    emb = _rms(jnp.dot(p_ref[...].astype(BF16), wp_ref[...], preferred_element_type=F32), gout_ref[...])
    h3 = h2 + gate * emb
    y_ref[...] = _rms(h3, gfin_ref[...]) if final else h3


def _ple(h1, moe, row_off, p, g_in, w_gate_b, w_proj_b, g_out, g_final, final, tm=256):
    n = p.shape[0]
    tm = _row_tile(n, tm)
    assert row_off % tm == 0
    off = row_off // tm
    row = lambda i: (i, 0)
    row_o = lambda i: (i + off, 0)
    fixed = lambda i: (0, 0)
    return pl.pallas_call(
        functools.partial(_ple_kernel, final=final),
        grid=(n // tm,),
        in_specs=[pl.BlockSpec((tm, D_MODEL), row),
                  pl.BlockSpec((tm, D_MODEL), row_o),
                  pl.BlockSpec((tm, PLE_DIM), row),
                  pl.BlockSpec((1, D_MODEL), fixed),
                  pl.BlockSpec((D_MODEL, D_MODEL), fixed),
                  pl.BlockSpec((PLE_DIM, D_MODEL), fixed),
                  pl.BlockSpec((1, D_MODEL), fixed),
                  pl.BlockSpec((1, D_MODEL), fixed)],
        out_specs=pl.BlockSpec((tm, D_MODEL), row),
        out_shape=jax.ShapeDtypeStruct((n, D_MODEL), F32),
        compiler_params=_cparams(("parallel",)),
        name="ple",
    )(h1, moe, p, g_in[None, :], w_gate_b, w_proj_b, g_out[None, :], g_final[None, :])


def kernel(x_prompt, x_sample, cache_k, cache_v, state_ssm, state_conv, page_table, p_prompt, p_sample, g_mix, w_in, w_out, g_attn, sb_bias, conv_w, conv_b, dt_bias, a_log, d_skip, g_ssd, g_ffn, w_router, b_router, w_gate_up, b_gate_up, w_down, b_down, g_ple_in, w_ple_gate, w_ple_proj, g_ple_out, g_final):
    depth = w_in.shape[0]
    bp, tp, _ = x_prompt.shape
    bs, ts, _ = x_sample.shape
    n_p, n_s = bp * tp, bs * ts
    h_p = x_prompt.reshape(n_p, D_MODEL)
    h_s = x_sample.reshape(n_s, D_MODEL)
    outs = [[] for _ in range(8)]

    for l in range(depth):
        w_in_pad = jnp.concatenate(
            [w_in[l], jnp.zeros((D_MODEL, _IN_COLS - w_in.shape[2]), w_in.dtype)], axis=1).astype(BF16)
        w_out_b = w_out[l].astype(BF16)
        w_r = jnp.concatenate([w_router[l], jnp.zeros((D_MODEL, LANES - N_EXPERTS), F32)], axis=1)
        w_router3 = jnp.stack(_split3(w_r))
        b_router_pad = jnp.concatenate([b_router[l], jnp.full((LANES - N_EXPERTS,), NEG_BIG, F32)])[None, :]
        w_pg = w_ple_gate[l].astype(BF16)
        w_pp = w_ple_proj[l].astype(BF16)

        def mixer_stage(h, b, t, past):
            q, k, v, z, xbc, dt_raw = _inproj(h, g_mix[l][None, :], w_in_pad)
            r3 = lambda a: a.reshape(b, t, a.shape[-1])
            if past is None:
                o = _attn_prompt(r3(q), r3(k), r3(v), sb_bias[l])
                h0 = jnp.zeros((b, SSD_WIDTH, SSD_STATE), F32)
                conv0 = jnp.zeros((b, SUBLANES, CONV_DIM), F32)
            else:
                o = _attn_sample(r3(q), r3(k), r3(v), cache_k[l], cache_v[l], page_table, sb_bias[l])
                h0 = state_ssm[l].reshape(b, SSD_WIDTH, SSD_STATE)
                conv0 = jnp.concatenate(
                    [jnp.zeros((b, SUBLANES - (SSD_CONV - 1), CONV_DIM), F32), state_conv[l]], axis=1)
            y, h_fin = _ssd(r3(xbc), r3(dt_raw), h0, conv0, conv_w[l], conv_b[l], dt_bias[l], a_log[l], d_skip[l])
            conv_new = jnp.concatenate([conv0, r3(xbc)], axis=1)[:, -(SSD_CONV - 1):]
            h1, f, logits = _mixout(o.reshape(b * t, SB_WIDTH), y.reshape(b * t, SSD_WIDTH), z, h,
                                    g_attn[l], g_ssd[l], w_out_b, g_ffn[l], w_router3, b_router_pad)
            k_new = k.reshape(b, t, SB_HEADS, SB_HEAD_DIM)
            v_new = v.reshape(b, t, SB_HEADS, SB_HEAD_DIM)
            ssm_new = h_fin.reshape(b, SSD_HEADS, SSD_HEAD_DIM, SSD_STATE)
            return h1, f, logits, (k_new, v_new, ssm_new, conv_new)

        h1_p, f_p, lg_p, new_p = mixer_stage(h_p, bp, tp, None)
        h1_s, f_s, lg_s, new_s = mixer_stage(h_s, bs, ts, True)
        f = jnp.concatenate([f_p, f_s], axis=0)
        logits = jnp.concatenate([lg_p, lg_s], axis=0)
        moe = _moe(f, logits, w_gate_up[l], b_gate_up[l], w_down[l], b_down[l])
        last = l == depth - 1
        h_p = _ple(h1_p, moe, 0, p_prompt[l].reshape(n_p, PLE_DIM), g_ple_in[l], w_pg, w_pp, g_ple_out[l], g_final, last)
        h_s = _ple(h1_s, moe, n_p, p_sample[l].reshape(n_s, PLE_DIM), g_ple_in[l], w_pg, w_pp, g_ple_out[l], g_final, last)
        for dst, val in zip(outs, new_p + new_s):
            dst.append(val)

    y_prompt = h_p.reshape(bp, tp, D_MODEL)
    y_sample = h_s.reshape(bs, ts, D_MODEL)
    return (y_prompt, y_sample) + tuple(jnp.stack(o) for o in outs)
```

```python
import functools

import jax
import jax.numpy as jnp
from jax import lax
from jax.experimental import pallas as pl
from jax.experimental.pallas import tpu as pltpu

F32 = jnp.float32
BF16 = jnp.bfloat16

D_MODEL = 1024
SB_HEADS = 8
SB_HEAD_DIM = 64
SB_WIDTH = SB_HEADS * SB_HEAD_DIM
SB_SCALE = SB_HEAD_DIM ** -0.5
SSD_HEADS = 8
SSD_HEAD_DIM = 64
SSD_WIDTH = SSD_HEADS * SSD_HEAD_DIM
SSD_GROUPS = 2
SSD_STATE = 128
SSD_CONV = 4
CONV_DIM = SSD_WIDTH + 2 * SSD_GROUPS * SSD_STATE
N_EXPERTS = 32
TOP_K = 4
D_EXPERT = D_MODEL
SWIGLU_LIMIT = 7.0
SWIGLU_ALPHA = 1.702
PLE_DIM = 256
EPS = 1e-6

LANES = 128
SUBLANES = 8
VMEM_LIMIT = 56 * 1024 * 1024
NEG_BIG = -1e30

_DT_PAD = LANES
_IN_CUTS = (SB_WIDTH, SB_WIDTH, SB_WIDTH, SSD_WIDTH, CONV_DIM, _DT_PAD)
_IN_COLS = sum(_IN_CUTS)

_NT = (((1,), (1,)), ((), ()))
_TN = (((0,), (0,)), ((), ()))


def _row_tile(n, tm):
    tm = min(tm, n)
    while n % tm:
        tm -= SUBLANES
    return tm


def _cparams(sem):
    return pltpu.CompilerParams(dimension_semantics=sem, vmem_limit_bytes=VMEM_LIMIT)


def _rms(x, g):
    return x * lax.rsqrt(jnp.mean(x * x, axis=-1, keepdims=True) + EPS) * g


def _split3(x):
    hi = x.astype(BF16)
    r1 = x - hi.astype(F32)
    mid = r1.astype(BF16)
    lo = (r1 - mid.astype(F32)).astype(BF16)
    return hi, mid, lo


def _dot3_l(x, m):
    hi, mid, lo = _split3(x)
    d = lambda a: jnp.dot(a, m, preferred_element_type=F32)
    return d(hi) + d(mid) + d(lo)


def _dot3_r(m, x):
    hi, mid, lo = _split3(x)
    d = lambda a: jnp.dot(m, a, preferred_element_type=F32)
    return d(hi) + d(mid) + d(lo)


def _inproj_kernel(x_ref, g_ref, w_ref, q_ref, k_ref, v_ref, z_ref, xbc_ref, dt_ref):
    a = _rms(x_ref[...], g_ref[...]).astype(BF16)
    c0 = 0
    for ref, width in zip((q_ref, k_ref, v_ref, z_ref, xbc_ref, dt_ref), _IN_CUTS):
        ref[...] = jnp.dot(a, w_ref[:, c0:c0 + width], preferred_element_type=F32)
        c0 += width


def _inproj(x, g, w_pad, tm=256):
    n = x.shape[0]
    tm = _row_tile(n, tm)
    row = lambda i: (i, 0)
    fixed = lambda i: (0, 0)
    return pl.pallas_call(
        _inproj_kernel,
        grid=(n // tm,),
        in_specs=[pl.BlockSpec((tm, D_MODEL), row),
                  pl.BlockSpec((1, D_MODEL), fixed),
                  pl.BlockSpec((D_MODEL, _IN_COLS), fixed)],
        out_specs=[pl.BlockSpec((tm, c), row) for c in _IN_CUTS],
        out_shape=[jax.ShapeDtypeStruct((n, c), F32) for c in _IN_CUTS],
        compiler_params=_cparams(("parallel",)),
        name="inproj",
    )(x, g, w_pad)


def _softplus(z):
    return jnp.maximum(z, 0.0) + jnp.log(1.0 + jnp.exp(-jnp.abs(z)))


def _attn_prompt_kernel(bias_ref, q_ref, k_ref, v_ref, tri_ref, o_ref, acc_ref, car_ref, *, blk):
    hp = pl.program_id(1)
    qi = pl.program_id(2)
    lane = lax.broadcasted_iota(jnp.int32, (blk, LANES), 1)
    first = lane < SB_HEAD_DIM
    q2 = q_ref[0] * SB_SCALE
    qs = jnp.concatenate([jnp.where(first, q2, 0.0), jnp.where(first, 0.0, q2)], axis=0).astype(BF16)
    rows = lax.broadcasted_iota(jnp.int32, (2 * blk, 1), 0)
    bias = jnp.where(rows < blk, bias_ref[hp * 2], bias_ref[hp * 2 + 1])
    tri = tri_ref[...]

    def block(kj, acc, car, causal):
        start = pl.multiple_of(kj * blk, blk)
        kb = k_ref[0, pl.ds(start, blk), :].astype(BF16)
        vb = v_ref[0, pl.ds(start, blk), :].astype(BF16)
        s = lax.dot_general(qs, kb, _NT, preferred_element_type=F32)
        sp = _softplus(s + bias)
        if causal is not None:
            sp = jnp.where(causal, sp, 0.0)
        later = jnp.dot(sp.astype(BF16), tri, preferred_element_type=F32)
        w = jnp.exp((s + (bias - car)) - sp - later)
        if causal is not None:
            w = jnp.where(causal, w, 0.0)
        acc = acc + jnp.dot(w.astype(BF16), vb, preferred_element_type=F32)
        car = car + jnp.sum(sp, axis=-1, keepdims=True)
        return acc, car

    r = lax.broadcasted_iota(jnp.int32, (2 * blk, blk), 0)
    c = lax.broadcasted_iota(jnp.int32, (2 * blk, blk), 1)
    causal = c < jnp.where(r < blk, r, r - blk)
    acc, car = block(qi, jnp.zeros((2 * blk, LANES), F32), jnp.zeros((2 * blk, 1), F32), causal)
    acc_ref[...] = acc
    car_ref[...] = car

    def run(first_block, count):
        acc, car = acc_ref[...], car_ref[...]
        for u in range(count):
            acc, car = block(first_block - u, acc, car, None)
        acc_ref[...] = acc
        car_ref[...] = car

    unroll = 4
    n_main = qi // unroll

    def main(t, carry):
        run(qi - 1 - unroll * t, unroll)
        return carry

    def tail(t, carry):
        run(qi - 1 - unroll * n_main - t, 1)
        return carry

    lax.fori_loop(0, n_main, main, 0)
    lax.fori_loop(0, qi - unroll * n_main, tail, 0)

    o_ref[0] = jnp.where(first, acc_ref[0:blk, :], acc_ref[blk:, :])


def _strict_upper_sum_matrix(n):
    j = lax.broadcasted_iota(jnp.int32, (n, n), 0)
    s = lax.broadcasted_iota(jnp.int32, (n, n), 1)
    return (j > s).astype(BF16)


def _attn_prompt(q, k, v, bias, blk=256):
    b, t, _ = q.shape
    blk = min(blk, t)
    tri = _strict_upper_sum_matrix(blk)
    grid_spec = pltpu.PrefetchScalarGridSpec(
        num_scalar_prefetch=0,
        grid=(b, SB_WIDTH // LANES, t // blk),
        in_specs=[pl.BlockSpec(memory_space=pltpu.SMEM),
                  pl.BlockSpec((1, blk, LANES), lambda bi, h, i: (bi, i, h)),
                  pl.BlockSpec((1, t, LANES), lambda bi, h, i: (bi, 0, h)),
                  pl.BlockSpec((1, t, LANES), lambda bi, h, i: (bi, 0, h)),
                  pl.BlockSpec((blk, blk), lambda bi, h, i: (0, 0))],
        out_specs=pl.BlockSpec((1, blk, LANES), lambda bi, h, i: (bi, i, h)),
        scratch_shapes=[pltpu.VMEM((2 * blk, LANES), F32), pltpu.VMEM((2 * blk, 1), F32)],
    )
    return pl.pallas_call(
        functools.partial(_attn_prompt_kernel, blk=blk),
        grid_spec=grid_spec,
        out_shape=jax.ShapeDtypeStruct((b, t, SB_WIDTH), F32),
        compiler_params=_cparams(("parallel", "parallel", "arbitrary")),
        name="attn_prompt",
    )(bias, q, k, v, tri)


def _attn_sample_kernel(pt_ref, q_ref, kn_ref, vn_ref, bias_ref, trio_ref, *rest, pps, page, tq):
    del pt_ref
    kp_refs = rest[:pps]
    vp_refs = rest[pps:2 * pps]
    o_ref = rest[2 * pps]
    qbd_ref, acc_ref, car_ref = rest[2 * pps + 1:]
    j = pl.program_id(1)
    nj = pl.num_programs(1)
    nq = SB_HEADS * tq

    def attend(kts, vts, mask):
        n = len(kts)
        kcat = jnp.concatenate(kts, axis=1).astype(BF16)
        bias = jnp.concatenate([bias_ref[...]] * n, axis=1)
        z = jnp.dot(qbd_ref[...], kcat, preferred_element_type=F32) + bias
        sp = _softplus(z)
        if mask is not None:
            sp = jnp.where(mask, sp, 0.0)
        sp_rows = jnp.concatenate([sp[:, i * page:(i + 1) * page] for i in range(n)], axis=0)
        hi = sp_rows.astype(BF16)
        mid = (sp_rows - hi.astype(F32)).astype(BF16)
        sums = jnp.dot(jnp.concatenate([hi, mid], axis=1), trio_ref[...], preferred_element_type=F32)
        car = car_ref[...]
        later = []
        for i in range(n):
            part = sums[i * nq:(i + 1) * nq, :]
            later.append(part[:, :page] + car)
            car = car + part[:, page:]
        car_ref[...] = car
        w = jnp.exp(z - sp - jnp.concatenate(later, axis=1))
        if mask is not None:
            w = jnp.where(mask, w, 0.0)
        vcat = jnp.concatenate(vts, axis=1).astype(BF16)
        acc_ref[...] += lax.dot_general(w.astype(BF16), vcat, _NT, preferred_element_type=F32)

    @pl.when(j == 0)
    def _():
        q = q_ref[0] * SB_SCALE
        qt = jnp.concatenate([q] * SB_HEADS, axis=0)
        r = lax.broadcasted_iota(jnp.int32, (nq, SB_WIDTH), 0)
        c = lax.broadcasted_iota(jnp.int32, (nq, SB_WIDTH), 1)
        qbd_ref[...] = jnp.where(r // tq == c // SB_HEAD_DIM, qt, 0.0).astype(BF16)
        acc_ref[...] = jnp.zeros_like(acc_ref)
        car_ref[...] = jnp.zeros_like(car_ref)
        fill = jnp.zeros((page - tq, SB_WIDTH), F32)
        kn_t = jnp.concatenate([kn_ref[0], fill], axis=0).T
        vn_t = jnp.concatenate([vn_ref[0], fill], axis=0).T
        query = lax.broadcasted_iota(jnp.int32, (nq, page), 0) % tq
        key = lax.broadcasted_iota(jnp.int32, (nq, page), 1)
        attend([kn_t], [vn_t], key < query)

    flat = lambda ref: ref[0].reshape(SB_WIDTH, page)
    attend([flat(r) for r in kp_refs], [flat(r) for r in vp_refs], None)

    @pl.when(j == nj - 1)
    def _():
        lane = lax.broadcasted_iota(jnp.int32, (tq, SB_WIDTH), 1)
        out = jnp.zeros((tq, SB_WIDTH), F32)
        for h in range(SB_HEADS):
            rows = acc_ref[h * tq:(h + 1) * tq, :]
            out = jnp.where(lane // SB_HEAD_DIM == h, rows, out)
        o_ref[0] = out


def _attn_sample(q, k_new, v_new, cache_k, cache_v, page_table, bias, pps=8):
    bs, tq, _ = q.shape
    n_pool, page = cache_k.shape[0], cache_k.shape[1]
    n_pages = page_table.shape[1]
    pps = min(pps, n_pages)
    nq = SB_HEADS * tq
    assert n_pages % pps == 0 and tq % SUBLANES == 0
    ck = jnp.transpose(cache_k, (0, 2, 3, 1))
    cv = jnp.transpose(cache_v, (0, 2, 3, 1))
    bias_rows = jnp.broadcast_to(jnp.repeat(bias, tq)[:, None], (nq, page))
    jj = lax.broadcasted_iota(jnp.int32, (page, page), 0)
    ss = lax.broadcasted_iota(jnp.int32, (page, page), 1)
    half = jnp.concatenate([(jj > ss).astype(BF16), jnp.ones((page, page), BF16)], axis=1)
    trio = jnp.concatenate([half, half], axis=0)

    def page_spec(i):
        return pl.BlockSpec((1, SB_HEADS, SB_HEAD_DIM, page),
                            lambda b, j, pt: (pt[b, n_pages - 1 - (j * pps + i)], 0, 0, 0))

    per_seq = pl.BlockSpec((1, tq, SB_WIDTH), lambda b, j, pt: (b, 0, 0))
    grid_spec = pltpu.PrefetchScalarGridSpec(
        num_scalar_prefetch=1,
        grid=(bs, n_pages // pps),
        in_specs=[per_seq, per_seq, per_seq,
                  pl.BlockSpec((nq, page), lambda b, j, pt: (0, 0)),
                  pl.BlockSpec((2 * page, 2 * page), lambda b, j, pt: (0, 0))]
                 + [page_spec(i) for i in range(pps)] * 2,
        out_specs=per_seq,
        scratch_shapes=[pltpu.VMEM((nq, SB_WIDTH), BF16),
                        pltpu.VMEM((nq, SB_WIDTH), F32),
                        pltpu.VMEM((nq, page), F32)],
    )
    return pl.pallas_call(
        functools.partial(_attn_sample_kernel, pps=pps, page=page, tq=tq),
        grid_spec=grid_spec,
        out_shape=jax.ShapeDtypeStruct((bs, tq, SB_WIDTH), F32),
        compiler_params=_cparams(("parallel", "arbitrary")),
        name="attn_sample",
    )(page_table, q, k_new, v_new, bias_rows, trio, *([ck] * pps), *([cv] * pps))


def _ssd_kernel(xbc_ref, dt_ref, h0_ref, c0_ref, cw_ref, cb_ref, dtb_ref, a_ref, dsk_ref,
                tril_ref, triu_ref, y_ref, hout_ref, xext_ref, h_ref, *, chunk, tin):
    ci = pl.program_id(1)
    nc = pl.num_programs(1)
    pad = SUBLANES

    @pl.when(ci == 0)
    def _():
        h_ref[...] = h0_ref[0]
        xext_ref[0:pad, :] = c0_ref[0]

    @pl.when(ci > 0)
    def _():
        xext_ref[0:pad, :] = xext_ref[chunk:chunk + pad, :]

    if tin < chunk:
        xext_ref[pad:, :] = jnp.zeros((chunk, CONV_DIM), F32)
    xext_ref[pad:pad + tin, :] = xbc_ref[0]

    conv = cb_ref[...] + sum(
        xext_ref[pad - (SSD_CONV - 1) + jj: pad - (SSD_CONV - 1) + jj + chunk, :] * cw_ref[jj:jj + 1, :]
        for jj in range(SSD_CONV))
    u = conv * jax.nn.sigmoid(conv)

    if tin < chunk:
        dt_in = jnp.concatenate([dt_ref[0], jnp.zeros((chunk - tin, LANES), F32)], axis=0)
    else:
        dt_in = dt_ref[0]
    dt = jax.nn.softplus(dt_in + dtb_ref[...])
    if tin < chunk:
        live = lax.broadcasted_iota(jnp.int32, (chunk, LANES), 0) < tin
        dt = jnp.where(live, dt, 0.0)
    a = dt * a_ref[...]
    acs = _dot3_r(tril_ref[...], a)
    acs_t = _dot3_l(a.T, triu_ref[...])
    acs_last = acs[chunk - 1:chunk, :]
    decay_in = jnp.exp(acs_last - acs)
    e_acs = jnp.exp(acs)
    cdec_t = jnp.exp(acs_t[:, chunk - 1:chunk])

    lrow = lax.broadcasted_iota(jnp.int32, (chunk, chunk), 0)
    scol = lax.broadcasted_iota(jnp.int32, (chunk, chunk), 1)
    causal = scol <= lrow
    lane = lax.broadcasted_iota(jnp.int32, (chunk, LANES), 1)
    first = lane < SSD_HEAD_DIM
    srow = lax.broadcasted_iota(jnp.int32, (LANES, LANES), 0)

    hpg = SSD_HEADS // SSD_GROUPS
    for g in range(SSD_GROUPS):
        b0 = SSD_WIDTH + g * SSD_STATE
        c0 = SSD_WIDTH + SSD_GROUPS * SSD_STATE + g * SSD_STATE
        bm = u[:, b0:b0 + SSD_STATE].astype(BF16)
        cm = u[:, c0:c0 + SSD_STATE].astype(BF16)
        cb = lax.dot_general(cm, bm, _NT, preferred_element_type=F32)
        for pr in range(hpg // 2):
            h0i = g * hpg + 2 * pr
            h1i = h0i + 1
            x2 = u[:, h0i * SSD_HEAD_DIM:(h0i + 2) * SSD_HEAD_DIM]
            pick = lambda m: jnp.where(first, m[:, h0i:h0i + 1], m[:, h1i:h1i + 1])
            xdt = x2 * pick(dt)
            xdt_b = xdt.astype(BF16)
            ys = []
            for hi in (h0i, h1i):
                seg = acs[:, hi:hi + 1] - acs_t[hi:hi + 1, :]
                lmat = jnp.exp(jnp.where(causal, seg, -jnp.inf))
                ys.append(jnp.dot((cb * lmat).astype(BF16), xdt_b, preferred_element_type=F32))
            y_diag = jnp.where(first, ys[0], ys[1])
            hrows = slice(h0i * SSD_HEAD_DIM, (h0i + 2) * SSD_HEAD_DIM)
            h_prev = h_ref[hrows, :]
            y_off = lax.dot_general(cm, h_prev.astype(BF16), _NT, preferred_element_type=F32) * pick(e_acs)
            states = lax.dot_general((xdt * pick(decay_in)).astype(BF16), bm, _TN,
                                     preferred_element_type=F32)
            cdec = jnp.where(srow < SSD_HEAD_DIM, cdec_t[h0i:h0i + 1, :], cdec_t[h1i:h1i + 1, :])
            h_ref[hrows, :] = cdec * h_prev + states
            y2 = y_diag + y_off + dsk_ref[:, h0i * SSD_HEAD_DIM:(h0i + 2) * SSD_HEAD_DIM] * x2
            y_ref[0, :, h0i * SSD_HEAD_DIM:(h0i + 2) * SSD_HEAD_DIM] = y2[0:tin, :]

    @pl.when(ci == nc - 1)
    def _():
        hout_ref[0] = h_ref[...]


def _ssd(xbc, dt_raw, h0, conv0, conv_w, conv_b, dt_bias, a_log, d_skip, chunk=128):
    b, t, _ = xbc.shape
    tin = min(chunk, t)
    nc = t // tin
    pad_lanes = lambda v: jnp.concatenate([v.astype(F32), jnp.zeros((LANES - v.shape[0],), F32)])[None, :]
    dtb = pad_lanes(dt_bias)
    a_neg = pad_lanes(-jnp.exp(a_log.astype(F32)))
    dsk = jnp.repeat(d_skip.astype(F32), SSD_HEAD_DIM)[None, :]
    r = lax.broadcasted_iota(jnp.int32, (chunk, chunk), 0)
    c = lax.broadcasted_iota(jnp.int32, (chunk, chunk), 1)
    tril = (c <= r).astype(BF16)
    triu = (r <= c).astype(BF16)
    fixed = lambda bi, ci: (0, 0)
    per_b = lambda bi, ci: (bi, 0, 0)
    y, h_fin = pl.pallas_call(
        functools.partial(_ssd_kernel, chunk=chunk, tin=tin),
        grid=(b, nc),
        in_specs=[pl.BlockSpec((1, tin, CONV_DIM), lambda bi, ci: (bi, ci, 0)),
                  pl.BlockSpec((1, tin, LANES), lambda bi, ci: (bi, ci, 0)),
                  pl.BlockSpec((1, SSD_WIDTH, SSD_STATE), per_b),
                  pl.BlockSpec((1, SUBLANES, CONV_DIM), per_b),
                  pl.BlockSpec((SSD_CONV, CONV_DIM), fixed),
                  pl.BlockSpec((1, CONV_DIM), fixed),
                  pl.BlockSpec((1, LANES), fixed),
                  pl.BlockSpec((1, LANES), fixed),
                  pl.BlockSpec((1, SSD_WIDTH), fixed),
                  pl.BlockSpec((chunk, chunk), fixed),
                  pl.BlockSpec((chunk, chunk), fixed)],
        out_specs=[pl.BlockSpec((1, tin, SSD_WIDTH), lambda bi, ci: (bi, ci, 0)),
                   pl.BlockSpec((1, SSD_WIDTH, SSD_STATE), per_b)],
        out_shape=[jax.ShapeDtypeStruct((b, t, SSD_WIDTH), F32),
                   jax.ShapeDtypeStruct((b, SSD_WIDTH, SSD_STATE), F32)],
        scratch_shapes=[pltpu.VMEM((SUBLANES + chunk, CONV_DIM), F32),
                        pltpu.VMEM((SSD_WIDTH, SSD_STATE), F32)],
        compiler_params=_cparams(("parallel", "arbitrary")),
        name="ssd",
    )(xbc, dt_raw, h0, conv0, conv_w, conv_b[None, :], dtb, a_neg, dsk, tril, triu)
    return y, h_fin


def _mixout_kernel(o_ref, y_ref, z_ref, h_ref, ga_ref, gs_ref, wo_ref, gf_ref, wr_ref, br_ref,
                   h1_ref, f_ref, logit_ref):
    attn = _rms(o_ref[...], ga_ref[...]).astype(BF16)
    z = z_ref[...]
    yg = y_ref[...] * (z * jax.nn.sigmoid(z))
    gw = SSD_WIDTH // SSD_GROUPS
    parts = []
    for g in range(SSD_GROUPS):
        part = yg[:, g * gw:(g + 1) * gw]
        parts.append(part * lax.rsqrt(jnp.mean(part * part, axis=-1, keepdims=True) + EPS))
    yn = (jnp.concatenate(parts, axis=-1) * gs_ref[...]).astype(BF16)
    mix = (jnp.dot(attn, wo_ref[0:SB_WIDTH, :], preferred_element_type=F32)
           + jnp.dot(yn, wo_ref[SB_WIDTH:, :], preferred_element_type=F32))
    h1 = h_ref[...] + mix
    h1_ref[...] = h1
    f = _rms(h1, gf_ref[...])
    f_ref[...] = f.astype(BF16)

    f_hi, f_mid, f_lo = _split3(f)
    w_hi, w_mid, w_lo = wr_ref[0], wr_ref[1], wr_ref[2]
    d = lambda a, b: jnp.dot(a, b, preferred_element_type=F32)
    logit_ref[...] = (d(f_hi, w_hi) + (d(f_hi, w_mid) + d(f_mid, w_hi))
                      + (d(f_hi, w_lo) + d(f_mid, w_mid) + d(f_lo, w_hi))) + br_ref[...]


def _mixout(o, y, z, h, g_attn, g_ssd, w_out_b, g_ffn, w_router3, b_router_pad, tm=256):
    n = h.shape[0]
    tm = _row_tile(n, tm)
    row = lambda i: (i, 0)
    fixed = lambda i: (0, 0)
    return pl.pallas_call(
        _mixout_kernel,
        grid=(n // tm,),
        in_specs=[pl.BlockSpec((tm, SB_WIDTH), row),
                  pl.BlockSpec((tm, SSD_WIDTH), row),
                  pl.BlockSpec((tm, SSD_WIDTH), row),
                  pl.BlockSpec((tm, D_MODEL), row),
                  pl.BlockSpec((1, SB_WIDTH), fixed),
                  pl.BlockSpec((1, SSD_WIDTH), fixed),
                  pl.BlockSpec((SB_WIDTH + SSD_WIDTH, D_MODEL), fixed),
                  pl.BlockSpec((1, D_MODEL), fixed),
                  pl.BlockSpec((3, D_MODEL, LANES), lambda i: (0, 0, 0)),
                  pl.BlockSpec((1, LANES), fixed)],
        out_specs=[pl.BlockSpec((tm, D_MODEL), row),
                   pl.BlockSpec((tm, D_MODEL), row),
                   pl.BlockSpec((tm, LANES), row)],
        out_shape=[jax.ShapeDtypeStruct((n, D_MODEL), F32),
                   jax.ShapeDtypeStruct((n, D_MODEL), BF16),
                   jax.ShapeDtypeStruct((n, LANES), F32)],
        compiler_params=_cparams(("parallel",)),
        name="mixout",
    )(o, y, z, h, g_attn[None, :], g_ssd[None, :], w_out_b, g_ffn[None, :], w_router3, b_router_pad)


def _swiglu_packed(gu):
    half = gu.shape[1] // 2
    lane = lax.broadcasted_iota(jnp.int32, (gu.shape[0], half), 1)
    even = (lane % 2) == 0
    lo, hi = gu[:, :half], gu[:, half:]
    gate = jnp.where(even, lo, pltpu.roll(hi, 1, 1))
    up = jnp.where(even, pltpu.roll(lo, half - 1, 1), hi)
    gate = jnp.minimum(gate, SWIGLU_LIMIT)
    up = jnp.clip(up, -SWIGLU_LIMIT, SWIGLU_LIMIT)
    return (up + 1.0) * (gate * jax.nn.sigmoid(SWIGLU_ALPHA * gate))


MOE_TILE = 512
MOE_PIECE = 16
MOE_ROWS = 256


def _moe_pieces(tile):
    return tile * TOP_K // MOE_PIECE + N_EXPERTS


def _route_kernel(lg_ref, below_ref, before_ref, pos_ref, gate_ref, post_ref, cnt_ref):
    logits = lg_ref[...]
    tile = logits.shape[0]
    lane = lax.broadcasted_iota(jnp.int32, logits.shape, 1)
    work = logits
    tops, sels = [], []
    for _ in range(TOP_K):
        m = jnp.max(work, axis=-1, keepdims=True)
        idx = jnp.min(jnp.where(work == m, lane, LANES), axis=-1, keepdims=True)
        sel = lane == idx
        tops.append(m)
        sels.append(sel)
        work = jnp.where(sel, -jnp.inf, work)
    exps = [jnp.exp(t - tops[0]) for t in tops]
    denom = exps[0] + exps[1] + exps[2] + exps[3]

    member = jnp.zeros(logits.shape, F32)
    for sel in sels:
        member = jnp.where(sel, 1.0, member)
    rank = jnp.dot(below_ref[...], member.astype(BF16), preferred_element_type=F32)
    cnt = jnp.sum(member, axis=0, keepdims=True)
    pieces = jnp.floor((cnt + (MOE_PIECE - 1)) * (1.0 / MOE_PIECE))
    pieces8 = jnp.broadcast_to(pieces, (SUBLANES, LANES))
    start = jnp.dot(pieces8.astype(BF16), before_ref[...], preferred_element_type=F32)[0:1, :] * MOE_PIECE
    slot = start + rank
    pos = jnp.full(logits.shape, -1.0, F32)
    gate = jnp.zeros(logits.shape, F32)
    for k in range(TOP_K):
        pk = jnp.sum(jnp.where(sels[k], slot, 0.0), axis=-1, keepdims=True)
        pos = jnp.where(lane == k, pk, pos)
        gate = jnp.where(lane == k, exps[k] / denom, gate)
    pos_ref[...] = pos
    gate_ref[...] = gate
    post_ref[0] = pos.T[0:SUBLANES, :]
    cnt_ref[0] = pieces8


def _route(logits, tile):
    n = logits.shape[0]
    nt = n // tile
    r = lax.broadcasted_iota(jnp.int32, (tile, tile), 0)
    c = lax.broadcasted_iota(jnp.int32, (tile, tile), 1)
    below = (c < r).astype(BF16)
    e0 = lax.broadcasted_iota(jnp.int32, (LANES, LANES), 0)
    e1 = lax.broadcasted_iota(jnp.int32, (LANES, LANES), 1)
    before = (e0 < e1).astype(BF16)
    row = lambda i: (i, 0)
    fixed = lambda i: (0, 0)
    return pl.pallas_call(
        _route_kernel,
        grid=(nt,),
        in_specs=[pl.BlockSpec((tile, LANES), row),
                  pl.BlockSpec((tile, tile), fixed),
                  pl.BlockSpec((LANES, LANES), fixed)],
        out_specs=[pl.BlockSpec((tile, LANES), row),
                   pl.BlockSpec((tile, LANES), row),
                   pl.BlockSpec((1, SUBLANES, tile), lambda i: (i, 0, 0)),
                   pl.BlockSpec((1, SUBLANES, LANES), lambda i: (i, 0, 0))],
        out_shape=[jax.ShapeDtypeStruct((n, LANES), F32),
                   jax.ShapeDtypeStruct((n, LANES), F32),
                   jax.ShapeDtypeStruct((nt, SUBLANES, tile), F32),
                   jax.ShapeDtypeStruct((nt, SUBLANES, LANES), F32)],
        compiler_params=_cparams(("parallel",)),
        name="moe_route",
    )(logits, below, before)


def _piece_copy(src, src_row, dst, dst_row, sem):
    return pltpu.make_async_copy(src.at[pl.ds(pl.multiple_of(src_row, MOE_PIECE), MOE_PIECE)],
                                 dst.at[pl.ds(pl.multiple_of(dst_row, MOE_PIECE), MOE_PIECE)], sem)


def _start_pieces(copy, n):
    def pair(q, c):
        copy(2 * q).start(priority=0)
        copy(2 * q + 1).start(priority=1)
        return c

    lax.fori_loop(0, n // 2, pair, 0)

    @pl.when(n % 2 == 1)
    def _():
        copy(n - 1).start(priority=0)


def _moe_gather_kernel(dest_ref, np_ref, f_ref, post_ref, xs_in_ref, xs_ref, xl_ref, sem, *, pmax):
    del xs_in_ref
    i = pl.program_id(0)
    tile = f_ref.shape[0]
    rows = lax.broadcasted_iota(jnp.int32, (pmax * MOE_PIECE, tile), 0).astype(F32)
    post = post_ref[0]
    onehot = jnp.zeros(rows.shape, F32)
    for k in range(TOP_K):
        onehot = onehot + jnp.where(rows == post[k:k + 1, :], 1.0, 0.0)
    sorted_rows = jnp.dot(onehot.astype(BF16), f_ref[...], preferred_element_type=F32).astype(BF16)

    def copy(tile_idx, p):
        return _piece_copy(xl_ref, p * MOE_PIECE, xs_ref, dest_ref[tile_idx * pmax + p], sem)

    def wait_tile(tile_idx):
        def wait(p, c):
            copy(tile_idx, p).wait()
            return c
        lax.fori_loop(0, np_ref[tile_idx], wait, 0)

    @pl.when(i > 0)
    def _():
        wait_tile(i - 1)

    xl_ref[...] = sorted_rows
    _start_pieces(functools.partial(copy, i), np_ref[i])

    @pl.when(i == pl.num_programs(0) - 1)
    def _():
        wait_tile(i)


def _moe_gather(f, post, dest, npieces, total_rows, tile):
    n = f.shape[0]
    pmax = _moe_pieces(tile)
    xs0 = jnp.zeros((total_rows, D_MODEL), BF16)
    grid_spec = pltpu.PrefetchScalarGridSpec(
        num_scalar_prefetch=2,
        grid=(n // tile,),
        in_specs=[pl.BlockSpec((tile, D_MODEL), lambda i, d, c: (i, 0)),
                  pl.BlockSpec((1, SUBLANES, tile), lambda i, d, c: (i, 0, 0)),
                  pl.BlockSpec(memory_space=pl.ANY)],
        out_specs=pl.BlockSpec(memory_space=pl.ANY),
        scratch_shapes=[pltpu.VMEM((pmax * MOE_PIECE, D_MODEL), BF16), pltpu.SemaphoreType.DMA],
    )
    return pl.pallas_call(
        functools.partial(_moe_gather_kernel, pmax=pmax),
        grid_spec=grid_spec,
        out_shape=jax.ShapeDtypeStruct((total_rows, D_MODEL), BF16),
        input_output_aliases={4: 0},
        compiler_params=_cparams(("arbitrary",)),
        name="moe_gather",
    )(dest, npieces, f, post, xs0)


def _moe_expert_kernel(te_ref, tv_ref, xs_ref, wgu_ref, bgu_ref, wd_ref, bd_ref, ys_ref,
                       wgu_b, wd_f, wd_b):
    t = pl.program_id(0)
    e = te_ref[t]
    prev = te_ref[jnp.maximum(t - 1, 0)]

    @pl.when((t == 0) | (e != prev))
    def _():
        wgu_b[...] = wgu_ref[0].astype(BF16)
        half = D_EXPERT // 2
        for c in range(D_MODEL // LANES):
            cols = slice(c * LANES, (c + 1) * LANES)
            wd_f[c, pl.ds(0, half, stride=2), :] = wd_ref[0, 0:half, cols]
            wd_f[c, pl.ds(1, half, stride=2), :] = wd_ref[0, half:, cols]
            wd_b[:, cols] = wd_f[c].astype(BF16)

    @pl.when(tv_ref[t] == 1)
    def _():
        gu = jnp.dot(xs_ref[...], wgu_b[...], preferred_element_type=F32) + bgu_ref[0]
        act = _swiglu_packed(gu).astype(BF16)
        y = jnp.dot(act, wd_b[...], preferred_element_type=F32) + bd_ref[0]
        ys_ref[...] = y.astype(BF16)

    @pl.when(tv_ref[t] == 0)
    def _():
        ys_ref[...] = jnp.zeros_like(ys_ref)


def _moe_experts(xs, tile_expert, tile_valid, w_gate_up, b_gate_up, w_down, b_down):
    rows = xs.shape[0]
    per_e = lambda t, te, tv: (te[t], 0, 0)
    row = lambda t, te, tv: (t, 0)
    grid_spec = pltpu.PrefetchScalarGridSpec(
        num_scalar_prefetch=2,
        grid=(rows // MOE_ROWS,),
        in_specs=[pl.BlockSpec((MOE_ROWS, D_MODEL), row),
                  pl.BlockSpec((1, D_MODEL, 2 * D_EXPERT), per_e),
                  pl.BlockSpec((1, 1, 2 * D_EXPERT), per_e),
                  pl.BlockSpec((1, D_EXPERT, D_MODEL), per_e),
                  pl.BlockSpec((1, 1, D_MODEL), per_e)],
        out_specs=pl.BlockSpec((MOE_ROWS, D_MODEL), row),
        scratch_shapes=[pltpu.VMEM((D_MODEL, 2 * D_EXPERT), BF16),
                        pltpu.VMEM((D_MODEL // LANES, D_EXPERT, LANES), F32),
                        pltpu.VMEM((D_EXPERT, D_MODEL), BF16)],
    )
    return pl.pallas_call(
        _moe_expert_kernel,
        grid_spec=grid_spec,
        out_shape=jax.ShapeDtypeStruct((rows, D_MODEL), BF16),
        compiler_params=_cparams(("arbitrary",)),
        name="moe_experts",
    )(tile_expert, tile_valid, xs, w_gate_up, b_gate_up[:, None, :], w_down, b_down[:, None, :])


def _moe_combine_kernel(dest_ref, np_ref, pos_ref, gate_ref, ys_ref, o_ref, yl_ref, sem, *, pmax):
    i = pl.program_id(0)
    slot = i % 2

    def copy(tile_idx, p):
        s = tile_idx % 2
        return _piece_copy(ys_ref, dest_ref[tile_idx * pmax + p], yl_ref.at[s], p * MOE_PIECE, sem.at[s])

    def start_tile(tile_idx):
        _start_pieces(functools.partial(copy, tile_idx), np_ref[tile_idx])

    @pl.when(i == 0)
    def _():
        yl_ref[...] = jnp.zeros_like(yl_ref)
        start_tile(0)

    @pl.when(i + 1 < pl.num_programs(0))
    def _():
        start_tile(i + 1)

    def wait(p, c):
        copy(i, p).wait()
        return c

    lax.fori_loop(0, np_ref[i], wait, 0)

    tile = pos_ref.shape[0]
    cols = lax.broadcasted_iota(jnp.int32, (tile, pmax * MOE_PIECE), 1).astype(F32)
    pos = pos_ref[...]
    gate = gate_ref[...]
    weights = jnp.zeros(cols.shape, F32)
    for k in range(TOP_K):
        weights = weights + jnp.where(cols == pos[:, k:k + 1], gate[:, k:k + 1], 0.0)
    o_ref[...] = jnp.dot(weights.astype(BF16), yl_ref[slot], preferred_element_type=F32)


def _moe_combine(ys, pos, gate, dest, npieces, tile):
    n = pos.shape[0]
    pmax = _moe_pieces(tile)
    grid_spec = pltpu.PrefetchScalarGridSpec(
        num_scalar_prefetch=2,
        grid=(n // tile,),
        in_specs=[pl.BlockSpec((tile, LANES), lambda i, d, c: (i, 0)),
                  pl.BlockSpec((tile, LANES), lambda i, d, c: (i, 0)),
                  pl.BlockSpec(memory_space=pl.ANY)],
        out_specs=pl.BlockSpec((tile, D_MODEL), lambda i, d, c: (i, 0)),
        scratch_shapes=[pltpu.VMEM((2, pmax * MOE_PIECE, D_MODEL), BF16), pltpu.SemaphoreType.DMA((2,))],
    )
    return pl.pallas_call(
        functools.partial(_moe_combine_kernel, pmax=pmax),
        grid_spec=grid_spec,
        out_shape=jax.ShapeDtypeStruct((n, D_MODEL), F32),
        compiler_params=_cparams(("arbitrary",)),
        name="moe_combine",
    )(dest, npieces, pos, gate, ys)


def _moe_tile(n):
    tile = min(MOE_TILE, n)
    while n % tile or tile % MOE_PIECE:
        tile -= SUBLANES
    return tile


def _moe(f, logits, w_gate_up, b_gate_up, w_down, b_down):
    n = f.shape[0]
    tile = _moe_tile(n)
    nt = n // tile
    pmax = _moe_pieces(tile)
    pos, gate, post, cnt = _route(logits, tile)

    i32 = jnp.int32
    rows = cnt[:, 0, :N_EXPERTS].astype(i32) * MOE_PIECE
    lstart = jnp.cumsum(rows, axis=1) - rows
    group = jnp.sum(rows, axis=0)
    group_pad = (group + MOE_ROWS - 1) // MOE_ROWS * MOE_ROWS
    goff = jnp.cumsum(group_pad) - group_pad
    gbase = goff[None, :] + jnp.cumsum(rows, axis=0) - rows
    npieces = (jnp.sum(rows, axis=1) // MOE_PIECE).astype(i32)
    p_row = jnp.arange(pmax, dtype=i32) * MOE_PIECE
    p3 = p_row[None, :, None]
    owns = (p3 >= lstart[:, None, :]) & (p3 < (lstart + rows)[:, None, :])
    dest = jnp.sum(jnp.where(owns, (gbase - lstart)[:, None, :] + p3, 0), axis=-1).astype(i32).reshape(-1)

    n_tiles = (n * TOP_K + nt * N_EXPERTS * (MOE_PIECE - 1)) // MOE_ROWS + N_EXPERTS + 1
    total_rows = n_tiles * MOE_ROWS
    t_row = jnp.arange(n_tiles, dtype=i32) * MOE_ROWS
    gend = goff + group_pad
    tile_expert = jnp.minimum(jnp.sum(t_row[:, None] >= gend[None, :], axis=-1), N_EXPERTS - 1).astype(i32)
    tile_valid = (t_row < gend[-1]).astype(i32)

    xs = _moe_gather(f, post, dest, npieces, total_rows, tile)
    ys = _moe_experts(xs, tile_expert, tile_valid, w_gate_up, b_gate_up, w_down, b_down)
    return _moe_combine(ys, pos, gate, dest, npieces, tile)


def _ple_kernel(h1_ref, moe_ref, p_ref, gin_ref, wg_ref, wp_ref, gout_ref, gfin_ref, y_ref, *, final):
    h2 = h1_ref[...] + moe_ref[...]
    c = _rms(h2, gin_ref[...]).astype(BF16)
    gate = jax.nn.sigmoid(jnp.dot(c, wg_ref[...], preferred_element_type=F32))
    emb = _rms(jnp.dot(p_ref[...].astype(BF16), wp_ref[...], preferred_element_type=F32), gout_ref[...])
    h3 = h2 + gate * emb
    y_ref[...] = _rms(h3, gfin_ref[...]) if final else h3


def _ple(h1, moe, row_off, p, g_in, w_gate_b, w_proj_b, g_out, g_final, final, tm=256):
    n = p.shape[0]
    tm = _row_tile(n, tm)
    assert row_off % tm == 0
    off = row_off // tm
    row = lambda i: (i, 0)
    row_o = lambda i: (i + off, 0)
    fixed = lambda i: (0, 0)
    return pl.pallas_call(
        functools.partial(_ple_kernel, final=final),
        grid=(n // tm,),
        in_specs=[pl.BlockSpec((tm, D_MODEL), row),
                  pl.BlockSpec((tm, D_MODEL), row_o),
                  pl.BlockSpec((tm, PLE_DIM), row),
                  pl.BlockSpec((1, D_MODEL), fixed),
                  pl.BlockSpec((D_MODEL, D_MODEL), fixed),
                  pl.BlockSpec((PLE_DIM, D_MODEL), fixed),
                  pl.BlockSpec((1, D_MODEL), fixed),
                  pl.BlockSpec((1, D_MODEL), fixed)],
        out_specs=pl.BlockSpec((tm, D_MODEL), row),
        out_shape=jax.ShapeDtypeStruct((n, D_MODEL), F32),
        compiler_params=_cparams(("parallel",)),
        name="ple",
    )(h1, moe, p, g_in[None, :], w_gate_b, w_proj_b, g_out[None, :], g_final[None, :])


def kernel(x_prompt, x_sample, cache_k, cache_v, state_ssm, state_conv, page_table, p_prompt, p_sample, g_mix, w_in, w_out, g_attn, sb_bias, conv_w, conv_b, dt_bias, a_log, d_skip, g_ssd, g_ffn, w_router, b_router, w_gate_up, b_gate_up, w_down, b_down, g_ple_in, w_ple_gate, w_ple_proj, g_ple_out, g_final):
    depth = w_in.shape[0]
    bp, tp, _ = x_prompt.shape
    bs, ts, _ = x_sample.shape
    n_p, n_s = bp * tp, bs * ts
    h_p = x_prompt.reshape(n_p, D_MODEL)
    h_s = x_sample.reshape(n_s, D_MODEL)
    outs = [[] for _ in range(8)]

    for l in range(depth):
        w_in_pad = jnp.concatenate(
            [w_in[l], jnp.zeros((D_MODEL, _IN_COLS - w_in.shape[2]), w_in.dtype)], axis=1).astype(BF16)
        w_out_b = w_out[l].astype(BF16)
        w_r = jnp.concatenate([w_router[l], jnp.zeros((D_MODEL, LANES - N_EXPERTS), F32)], axis=1)
        w_router3 = jnp.stack(_split3(w_r))
        b_router_pad = jnp.concatenate([b_router[l], jnp.full((LANES - N_EXPERTS,), NEG_BIG, F32)])[None, :]
        w_pg = w_ple_gate[l].astype(BF16)
        w_pp = w_ple_proj[l].astype(BF16)

        def mixer_stage(h, b, t, past):
            q, k, v, z, xbc, dt_raw = _inproj(h, g_mix[l][None, :], w_in_pad)
            r3 = lambda a: a.reshape(b, t, a.shape[-1])
            if past is None:
                o = _attn_prompt(r3(q), r3(k), r3(v), sb_bias[l])
                h0 = jnp.zeros((b, SSD_WIDTH, SSD_STATE), F32)
                conv0 = jnp.zeros((b, SUBLANES, CONV_DIM), F32)
            else:
                o = _attn_sample(r3(q), r3(k), r3(v), cache_k[l], cache_v[l], page_table, sb_bias[l])
                h0 = state_ssm[l].reshape(b, SSD_WIDTH, SSD_STATE)
                conv0 = jnp.concatenate(
                    [jnp.zeros((b, SUBLANES - (SSD_CONV - 1), CONV_DIM), F32), state_conv[l]], axis=1)
            y, h_fin = _ssd(r3(xbc), r3(dt_raw), h0, conv0, conv_w[l], conv_b[l], dt_bias[l], a_log[l], d_skip[l])
            conv_new = jnp.concatenate([conv0, r3(xbc)], axis=1)[:, -(SSD_CONV - 1):]
            h1, f, logits = _mixout(o.reshape(b * t, SB_WIDTH), y.reshape(b * t, SSD_WIDTH), z, h,
                                    g_attn[l], g_ssd[l], w_out_b, g_ffn[l], w_router3, b_router_pad)
            k_new = k.reshape(b, t, SB_HEADS, SB_HEAD_DIM)
            v_new = v.reshape(b, t, SB_HEADS, SB_HEAD_DIM)
            ssm_new = h_fin.reshape(b, SSD_HEADS, SSD_HEAD_DIM, SSD_STATE)
            return h1, f, logits, (k_new, v_new, ssm_new, conv_new)

        h1_p, f_p, lg_p, new_p = mixer_stage(h_p, bp, tp, None)
        h1_s, f_s, lg_s, new_s = mixer_stage(h_s, bs, ts, True)
        f = jnp.concatenate([f_p, f_s], axis=0)
        logits = jnp.concatenate([lg_p, lg_s], axis=0)
        moe = _moe(f, logits, w_gate_up[l], b_gate_up[l], w_down[l], b_down[l])
        last = l == depth - 1
        h_p = _ple(h1_p, moe, 0, p_prompt[l].reshape(n_p, PLE_DIM), g_ple_in[l], w_pg, w_pp, g_ple_out[l], g_final, last)
        h_s = _ple(h1_s, moe, n_p, p_sample[l].reshape(n_s, PLE_DIM), g_ple_in[l], w_pg, w_pp, g_ple_out[l], g_final, last)
        for dst, val in zip(outs, new_p + new_s):
            dst.append(val)

    y_prompt = h_p.reshape(bp, tp, D_MODEL)
    y_sample = h_s.reshape(bs, ts, D_MODEL)
    return (y_prompt, y_sample) + tuple(jnp.stack(o) for o in outs)
```
